```python
import jax, jax.numpy as jnp
from jax import lax
import numpy as np

D_MODEL = 2048
BATCH = 4
SEQ = 4096
DEPTH = 2

BRANCH_WIDTH = D_MODEL // 2
N_BRANCHES = 4
EPS = 1e-6

SSD_HEAD_DIM = 64
SSD_HEADS = BRANCH_WIDTH // SSD_HEAD_DIM
SSD_GROUPS = 2
SSD_STATE = 128
SSD_CONV = 4
SSD_CHUNK = 128
SSD_XBC = BRANCH_WIDTH + 2 * SSD_GROUPS * SSD_STATE

FOX_HEAD_DIM = 64
FOX_HEADS = BRANCH_WIDTH // FOX_HEAD_DIM
Q_BLOCK = 128

GLA_HEADS = 4
GLA_KEY_DIM = 128
GLA_VAL_DIM = BRANCH_WIDTH // GLA_HEADS
GLA_GATE_RANK = 16
GLA_GATE_TAU = 16.0
GLA_CHUNK = 64

MLA_HEADS = 8
MLA_NOPE = 128
MLA_ROPE = 64
MLA_V = BRANCH_WIDTH // MLA_HEADS
MLA_Q_RANK = 512
MLA_KV_RANK = 512
ROPE_BASE = 10000.0

IN_SPLITS = (
    BRANCH_WIDTH, SSD_XBC, SSD_HEADS,
    BRANCH_WIDTH, BRANCH_WIDTH, BRANCH_WIDTH, FOX_HEADS, BRANCH_WIDTH,
    GLA_HEADS * GLA_KEY_DIM, GLA_HEADS * GLA_KEY_DIM, BRANCH_WIDTH, GLA_GATE_RANK, BRANCH_WIDTH,
    MLA_Q_RANK, MLA_KV_RANK, MLA_ROPE, BRANCH_WIDTH,
    N_BRANCHES * D_MODEL,
)
N_IN = sum(IN_SPLITS)

kernel_name = "hybrid_gated_parallel_mixer_trunk"


def rmsnorm(x, w):
    xf = x.astype(jnp.float32)
    y = xf * lax.rsqrt(jnp.mean(xf * xf, axis=-1, keepdims=True) + EPS)
    return (y * w.astype(jnp.float32)).astype(x.dtype)


def causal_depthwise_conv(x, w, b):
    ch = x.shape[-1]
    y = lax.conv_general_dilated(
        x, w[:, None, :].astype(x.dtype), window_strides=(1,),
        padding=[(SSD_CONV - 1, 0)], dimension_numbers=('NWC', 'WIO', 'NWC'),
        feature_group_count=ch)
    return y + b.astype(x.dtype)


def ssd_chunked(x, dt, a, b_in, c_in):
    bsz, s, h, p = x.shape
    g, n = b_in.shape[-2:]
    r = h // g
    L = SSD_CHUNK
    nc = s // L
    log_a = (dt * a).reshape(bsz, nc, L, g, r)
    xdt = (x.astype(jnp.float32) * dt[..., None]).reshape(bsz, nc, L, g, r, p)
    bc = b_in.astype(jnp.float32).reshape(bsz, nc, L, g, n)
    cc = c_in.astype(jnp.float32).reshape(bsz, nc, L, g, n)
    a_cum = jnp.cumsum(log_a, axis=2)
    seg = a_cum[:, :, :, None] - a_cum[:, :, None, :]
    causal = jnp.tril(jnp.ones((L, L), bool))[None, None, :, :, None, None]
    decay = jnp.where(causal, jnp.exp(jnp.minimum(seg, 0.0)), 0.0)
    cb = jnp.einsum('bclgn,bcsgn->bclsg', cc, bc)
    y_diag = jnp.einsum('bclsgr,bcsgrp->bclgrp', cb[..., None] * decay, xdt)
    decay_to_end = jnp.exp(a_cum[:, :, -1:] - a_cum)
    chunk_states = jnp.einsum('bclgn,bclgrp->bcgrpn', bc, xdt * decay_to_end[..., None])
    chunk_decay = jnp.exp(a_cum[:, :, -1])

    def step(state, inp):
        st, dec = inp
        return state * dec[..., None, None] + st, state

    h0 = jnp.zeros((bsz, g, r, p, n), jnp.float32)
    _, prev = lax.scan(step, h0, (chunk_states.transpose(1, 0, 2, 3, 4, 5),
                                  chunk_decay.transpose(1, 0, 2, 3)))
    prev = prev.transpose(1, 0, 2, 3, 4, 5)
    y_off = jnp.einsum('bclgn,bcgrpn->bclgrp', cc, prev) * jnp.exp(a_cum)[..., None]
    return (y_diag + y_off).reshape(bsz, s, h, p)


def gla_chunked(q, k, v, log_a):
    bsz, h, s, dk = q.shape
    dv = v.shape[-1]
    C = GLA_CHUNK
    nc = s // C

    def to_chunks(t):
        return t.astype(jnp.float32).reshape(bsz, h, nc, C, t.shape[-1]).transpose(2, 0, 1, 3, 4)

    causal = jnp.tril(jnp.ones((C, C), bool))[:, :, None]

    def step(state, inp):
        qc, kc, vc, gc = inp
        bcum = jnp.cumsum(gc, axis=2)
        diff = bcum[:, :, :, None, :] - bcum[:, :, None, :, :]
        decay = jnp.where(causal, jnp.exp(jnp.minimum(diff, 0.0)), 0.0)
        attn = jnp.einsum('bhijd,bhjd->bhij', qc[:, :, :, None, :] * decay, kc)
        o = (jnp.einsum('bhij,bhje->bhie', attn, vc)
             + jnp.einsum('bhid,bhde->bhie', qc * jnp.exp(bcum), state))
        b_last = bcum[:, :, -1:, :]
        state = (state * jnp.exp(b_last)[:, :, 0, :, None]
                 + jnp.einsum('bhjd,bhje->bhde', kc * jnp.exp(b_last - bcum), vc))
        return state, o

    s0 = jnp.zeros((bsz, h, dk, dv), jnp.float32)
    _, o = lax.scan(step, s0, (to_chunks(q), to_chunks(k), to_chunks(v), to_chunks(log_a)))
    return o.transpose(1, 2, 0, 3, 4).reshape(bsz, h, s, dv)


def causal_block_attention(q, k, v, scale, log_f_cum=None):
    bsz, h, s, dk = q.shape
    dv = v.shape[-1]
    nb = s // Q_BLOCK
    qb = q.reshape(bsz, h, nb, Q_BLOCK, dk).transpose(2, 0, 1, 3, 4)
    key_pos = jnp.arange(s)

    def one_block(args):
        i, q_blk = args
        start = i * Q_BLOCK
        logits = jnp.einsum('bhqd,bhkd->bhqk', q_blk, k).astype(jnp.float32) * scale
        if log_f_cum is not None:
            fq = lax.dynamic_slice_in_dim(log_f_cum, start, Q_BLOCK, axis=2)
            logits = logits + fq[..., :, None] - log_f_cum[..., None, :]
        q_pos = start + jnp.arange(Q_BLOCK)
        logits = jnp.where(key_pos[None, :] <= q_pos[:, None], logits, -jnp.inf)
        probs = jax.nn.softmax(logits, axis=-1).astype(v.dtype)
        return jnp.einsum('bhqk,bhkd->bhqd', probs, v)

    out = lax.map(one_block, (jnp.arange(nb), qb))
    return out.transpose(1, 2, 0, 3, 4).reshape(bsz, h, s, dv)


def apply_rope(t, cos, sin):
    half = MLA_ROPE // 2
    tf = t.astype(jnp.float32)
    t1, t2 = tf[..., :half], tf[..., half:]
    return jnp.concatenate([t1 * cos - t2 * sin, t1 * sin + t2 * cos], axis=-1).astype(t.dtype)


def hybrid_layer(x, positions, pre_norm, post_norm, w_in, conv_w, conv_b, dt_bias, a_log, d_skip,
                 ssm_norm, fgate_b, gla_w2, gla_b, gla_norm, q_norm, kv_norm, w_uq, w_ukv,
                 w_branch, w_out):
    bsz, s, _ = x.shape
    f32 = jnp.float32
    h = rmsnorm(x, pre_norm)
    proj = h @ w_in
    (m_z, m_xbc, m_dt, f_q, f_k, f_v, f_f, f_gate, g_q, g_k, g_v, g_lr, g_gate,
     l_cq, l_ckv, l_kr, l_gate, merge) = jnp.split(
        proj, np.cumsum(IN_SPLITS)[:-1].tolist(), axis=-1)

    def heads(t, nh):
        return t.reshape(bsz, s, nh, -1).transpose(0, 2, 1, 3)

    xbc = jax.nn.silu(causal_depthwise_conv(m_xbc, conv_w, conv_b))
    xs, bs, cs = jnp.split(xbc, [BRANCH_WIDTH, BRANCH_WIDTH + SSD_GROUPS * SSD_STATE], axis=-1)
    xs = xs.reshape(bsz, s, SSD_HEADS, SSD_HEAD_DIM)
    dt = jax.nn.softplus(m_dt.astype(f32) + dt_bias.astype(f32))
    a = -jnp.exp(a_log.astype(f32))
    y = ssd_chunked(xs, dt, a, bs.reshape(bsz, s, SSD_GROUPS, SSD_STATE),
                    cs.reshape(bsz, s, SSD_GROUPS, SSD_STATE))
    y = y + xs.astype(f32) * d_skip.astype(f32)[:, None]
    y = (y.reshape(bsz, s, BRANCH_WIDTH) * jax.nn.silu(m_z.astype(f32))).astype(x.dtype)
    out_a = rmsnorm(y.reshape(bsz, s, SSD_GROUPS, -1),
                    ssm_norm.reshape(SSD_GROUPS, -1)).reshape(bsz, s, BRANCH_WIDTH)

    log_f = jax.nn.log_sigmoid(f_f.astype(f32) + fgate_b.astype(f32))
    f_cum = jnp.cumsum(log_f, axis=1).transpose(0, 2, 1)
    o = causal_block_attention(heads(f_q, FOX_HEADS), heads(f_k, FOX_HEADS), heads(f_v, FOX_HEADS),
                               FOX_HEAD_DIM ** -0.5, f_cum)
    out_b = o.transpose(0, 2, 1, 3).reshape(bsz, s, BRANCH_WIDTH) * jax.nn.silu(f_gate)

    log_alpha = jax.nn.log_sigmoid((g_lr @ gla_w2 + gla_b).astype(f32)) / GLA_GATE_TAU
    o = gla_chunked(heads(g_q, GLA_HEADS).astype(f32) * GLA_KEY_DIM ** -0.5,
                    heads(g_k, GLA_HEADS), heads(g_v, GLA_HEADS), heads(log_alpha, GLA_HEADS))
    o = o.transpose(0, 2, 1, 3).astype(x.dtype)
    out_c = rmsnorm(o, gla_norm).reshape(bsz, s, BRANCH_WIDTH) * jax.nn.silu(g_gate)

    cq = rmsnorm(l_cq, q_norm)
    ckv = rmsnorm(l_ckv, kv_norm)
    q = (cq @ w_uq).reshape(bsz, s, MLA_HEADS, MLA_NOPE + MLA_ROPE)
    kv = (ckv @ w_ukv).reshape(bsz, s, MLA_HEADS, MLA_NOPE + MLA_V)
    inv_freq = 1.0 / (ROPE_BASE ** (jnp.arange(MLA_ROPE // 2, dtype=f32) * 2.0 / MLA_ROPE))
    ang = positions.astype(f32)[..., None] * inv_freq
    cos, sin = jnp.cos(ang), jnp.sin(ang)
    q_rope = apply_rope(q[..., MLA_NOPE:], cos[:, :, None], sin[:, :, None])
    k_rope = apply_rope(l_kr, cos, sin)
    q_full = jnp.concatenate([q[..., :MLA_NOPE], q_rope], axis=-1).transpose(0, 2, 1, 3)
    k_full = jnp.concatenate(
        [kv[..., :MLA_NOPE], jnp.broadcast_to(k_rope[:, :, None], (bsz, s, MLA_HEADS, MLA_ROPE))],
        axis=-1).transpose(0, 2, 1, 3)
    v = kv[..., MLA_NOPE:].transpose(0, 2, 1, 3)
    o = causal_block_attention(q_full, k_full, v, (MLA_NOPE + MLA_ROPE) ** -0.5)
    out_d = o.transpose(0, 2, 1, 3).reshape(bsz, s, BRANCH_WIDTH) * jax.nn.silu(l_gate)

    gates = jax.nn.sigmoid(merge.reshape(bsz, s, N_BRANCHES, D_MODEL))
    mixed = (gates[:, :, 0] * (out_a @ w_branch[0])
             + gates[:, :, 1] * (out_b @ w_branch[1])
             + gates[:, :, 2] * (out_c @ w_branch[2])
             + gates[:, :, 3] * (out_d @ w_branch[3]))
    return x + rmsnorm(mixed @ w_out, post_norm)


def setup_inputs(seed: int = 0) -> dict:
    key = jax.random.key(seed)
    ks = jax.random.split(key, 24)
    L = DEPTH
    nrm = jax.random.normal
    dt0 = jnp.exp(jax.random.uniform(ks[7], (L, SSD_HEADS)) * (np.log(0.1) - np.log(0.001)) + np.log(0.001))
    offsets = jax.random.randint(ks[1], (BATCH, 1), 0, 1024)
    return {
        'x': nrm(ks[0], (BATCH, SEQ, D_MODEL), jnp.float32),
        'positions': (offsets + jnp.arange(SEQ)[None, :]).astype(jnp.int32),
        'pre_norm': 1.0 + 0.02 * nrm(ks[2], (L, D_MODEL)),
        'post_norm': 1.0 + 0.02 * nrm(ks[3], (L, D_MODEL)),
        'w_in': nrm(ks[4], (L, D_MODEL, N_IN)) * D_MODEL ** -0.5,
        'conv_w': nrm(ks[5], (L, SSD_CONV, SSD_XBC)) * SSD_CONV ** -0.5,
        'conv_b': 0.02 * nrm(ks[6], (L, SSD_XBC)),
        'dt_bias': dt0 + jnp.log(-jnp.expm1(-dt0)),
        'a_log': jnp.log(jax.random.uniform(ks[8], (L, SSD_HEADS), minval=1.0, maxval=16.0)),
        'd_skip': 1.0 + 0.1 * nrm(ks[9], (L, SSD_HEADS)),
        'ssm_norm': 1.0 + 0.02 * nrm(ks[10], (L, BRANCH_WIDTH)),
        'fgate_b': jax.random.uniform(ks[11], (L, FOX_HEADS), minval=1.0, maxval=4.0),
        'gla_w2': nrm(ks[12], (L, GLA_GATE_RANK, GLA_HEADS * GLA_KEY_DIM)) * GLA_GATE_RANK ** -0.5,
        'gla_b': 0.02 * nrm(ks[13], (L, GLA_HEADS * GLA_KEY_DIM)),
        'gla_norm': 1.0 + 0.02 * nrm(ks[14], (L, GLA_VAL_DIM)),
        'q_norm': 1.0 + 0.02 * nrm(ks[15], (L, MLA_Q_RANK)),
        'kv_norm': 1.0 + 0.02 * nrm(ks[16], (L, MLA_KV_RANK)),
        'w_uq': nrm(ks[17], (L, MLA_Q_RANK, MLA_HEADS * (MLA_NOPE + MLA_ROPE))) * MLA_Q_RANK ** -0.5,
        'w_ukv': nrm(ks[18], (L, MLA_KV_RANK, MLA_HEADS * (MLA_NOPE + MLA_V))) * MLA_KV_RANK ** -0.5,
        'w_branch': nrm(ks[19], (L, N_BRANCHES, BRANCH_WIDTH, D_MODEL)) * BRANCH_WIDTH ** -0.5,
        'w_out': nrm(ks[20], (L, D_MODEL, D_MODEL)) * D_MODEL ** -0.5,
    }


def reference(x, positions, pre_norm, post_norm, w_in, conv_w, conv_b, dt_bias, a_log, d_skip,
              ssm_norm, fgate_b, gla_w2, gla_b, gla_norm, q_norm, kv_norm, w_uq, w_ukv,
              w_branch, w_out):
    for l in range(DEPTH):
        x = hybrid_layer(x, positions, pre_norm[l], post_norm[l], w_in[l], conv_w[l], conv_b[l],
                         dt_bias[l], a_log[l], d_skip[l], ssm_norm[l], fgate_b[l], gla_w2[l],
                         gla_b[l], gla_norm[l], q_norm[l], kv_norm[l], w_uq[l], w_ukv[l],
                         w_branch[l], w_out[l])
    return x
```

```python
import functools

import jax
import jax.numpy as jnp
import numpy as np
from jax import lax
from jax.experimental import pallas as pl
from jax.experimental.pallas import tpu as pltpu

F32 = jnp.float32
BF16 = jnp.bfloat16
HIGHEST = lax.Precision.HIGHEST

D_MODEL = 2048
BRANCH_WIDTH = 1024
N_BRANCHES = 4
EPS = 1e-6

SSD_HEAD_DIM = 64
SSD_HEADS = 16
SSD_GROUPS = 2
SSD_STATE = 128
SSD_CONV = 4
SSD_CHUNK = 128
SSD_XBC = 1536

FOX_HEAD_DIM = 64
FOX_HEADS = 16

GLA_HEADS = 4
GLA_KEY_DIM = 128
GLA_VAL_DIM = 256
GLA_GATE_RANK = 16
GLA_GATE_TAU = 16.0
GLA_CHUNK = 64
GLA_BAND = 16

MLA_HEADS = 8
MLA_NOPE = 128
MLA_ROPE = 64
MLA_V = 128
MLA_Q_RANK = 512
MLA_KV_RANK = 512
MLA_QK_PAD = 256
ROPE_BASE = 10000.0

LANES = 128
VMEM_LIMIT = 56 * 1024 * 1024

_IN_NAMES = ("m_z", "m_xbc", "m_dt", "f_q", "f_k", "f_v", "f_f", "f_gate",
             "g_q", "g_k", "g_v", "g_lr", "g_gate", "l_cq", "l_ckv", "l_kr", "l_gate", "merge")
_IN_SIZES = (1024, 1536, 16, 1024, 1024, 1024, 16, 1024,
             512, 512, 1024, 16, 1024, 512, 512, 64, 1024, 8192)
_BIG_ORDER = ("merge", "m_z", "f_q", "f_k", "f_v", "f_gate", "g_v", "g_gate", "l_gate",
              "g_q", "g_k", "l_cq", "l_ckv", "m_xbc")
_SIZE = dict(zip(_IN_NAMES, _IN_SIZES))
_OFF = {}
_o = 0
for _n in _BIG_ORDER:
    _OFF[_n] = _o
    _o += _SIZE[_n]
N_BIG = _o
N_SMALL = 3 * LANES
SM_DT, SM_FF, SM_LR = 0, 16, 32


def _cparams(sem):
    return pltpu.CompilerParams(dimension_semantics=sem, vmem_limit_bytes=VMEM_LIMIT)


def _silu(v):
    return v / (1.0 + jnp.exp(-v))


def _softplus(v):
    return jnp.maximum(v, 0.0) + jnp.log1p(jnp.exp(-jnp.abs(v)))


def _log_sigmoid(v):
    return jnp.minimum(v, 0.0) - jnp.log1p(jnp.exp(-jnp.abs(v)))


def _tril(n):
    r = lax.broadcasted_iota(jnp.int32, (n, n), 0)
    c = lax.broadcasted_iota(jnp.int32, (n, n), 1)
    return r >= c


def _dot(a, b):
    return jnp.dot(a, b, preferred_element_type=F32)


def _dot_nt(a, b):
    return lax.dot_general(a, b, (((1,), (1,)), ((), ())), preferred_element_type=F32)


def _dot_tn(a, b):
    return lax.dot_general(a, b, (((0,), (0,)), ((), ())), preferred_element_type=F32)


def _dot_hi(a, b):
    return jnp.dot(a, b, precision=HIGHEST, preferred_element_type=F32)


def _in_proj_kernel(x_ref, pn_ref, w_ref, ws_ref, o_ref, os_ref, h_ref):
    @pl.when(pl.program_id(1) == 0)
    def _():
        x = x_ref[...]
        ms = jnp.mean(x * x, axis=-1, keepdims=True)
        hb = (x * lax.rsqrt(ms + EPS) * pn_ref[...]).astype(BF16)
        h_ref[...] = hb
        os_ref[...] = _dot(hb, ws_ref[...])

    o_ref[...] = _dot(h_ref[...], w_ref[...]).astype(BF16)


def _in_proj(x2d, pre_norm, w_big, w_small, tm, tn):
    t = x2d.shape[0]
    return pl.pallas_call(
        _in_proj_kernel,
        grid=(t // tm, N_BIG // tn),
        in_specs=[
            pl.BlockSpec((tm, D_MODEL), lambda i, j: (i, 0)),
            pl.BlockSpec((1, D_MODEL), lambda i, j: (0, 0)),
            pl.BlockSpec((D_MODEL, tn), lambda i, j: (0, j)),
            pl.BlockSpec((D_MODEL, N_SMALL), lambda i, j: (0, 0)),
        ],
        out_specs=[
            pl.BlockSpec((tm, tn), lambda i, j: (i, j)),
            pl.BlockSpec((tm, N_SMALL), lambda i, j: (i, 0)),
        ],
        out_shape=[
            jax.ShapeDtypeStruct((t, N_BIG), BF16),
            jax.ShapeDtypeStruct((t, N_SMALL), F32),
        ],
        scratch_shapes=[pltpu.VMEM((tm, D_MODEL), BF16)],
        compiler_params=_cparams(("parallel", "arbitrary")),
        name="in_proj",
    )(x2d, pre_norm.reshape(1, D_MODEL), w_big, w_small)


def _fcum_kernel(s_ref, fb_ref, o_ref, ot_ref, carry_ref, *, tc):
    @pl.when(pl.program_id(1) == 0)
    def _():
        carry_ref[...] = jnp.zeros_like(carry_ref)

    lf = _log_sigmoid(s_ref[...] + fb_ref[...])
    cum = _dot_hi(_tril(tc).astype(F32), lf) + carry_ref[...]
    o_ref[...] = cum
    ot_ref[0] = cum.T
    carry_ref[...] = cum[tc - 1:tc, :]


def _fcum(small, fgate_b, bsz, s, tc):
    fb = jnp.zeros((1, LANES), F32).at[0, SM_FF:SM_FF + FOX_HEADS].set(fgate_b.astype(F32))
    nj = s // tc
    return pl.pallas_call(
        functools.partial(_fcum_kernel, tc=tc),
        grid=(bsz, nj),
        in_specs=[
            pl.BlockSpec((tc, LANES), lambda b, j: (b * nj + j, 0)),
            pl.BlockSpec((1, LANES), lambda b, j: (0, 0)),
        ],
        out_specs=[
            pl.BlockSpec((tc, LANES), lambda b, j: (b * nj + j, 0)),
            pl.BlockSpec((1, LANES, tc), lambda b, j: (b, 0, j)),
        ],
        out_shape=[
            jax.ShapeDtypeStruct((bsz * s, LANES), F32),
            jax.ShapeDtypeStruct((bsz, LANES, s), F32),
        ],
        scratch_shapes=[pltpu.VMEM((1, LANES), F32)],
        compiler_params=_cparams(("parallel", "arbitrary")),
        name="fcum",
    )(small, fb)


def _ssd_kernel(z_ref, xbc_ref, sm_ref, cw_ref, cb_ref, dtb_ref, alog_ref, e_ref, dsk_ref,
                nw_ref, o_ref, tail_ref, state_ref, y_ref):
    L = SSD_CHUNK
    gw = BRANCH_WIDTH // SSD_GROUPS
    hpg = SSD_HEADS // SSD_GROUPS

    @pl.when(pl.program_id(1) == 0)
    def _():
        tail_ref[...] = jnp.zeros_like(tail_ref)
        state_ref[...] = jnp.zeros_like(state_ref)

    cur = xbc_ref[...].astype(F32)
    xcat = jnp.concatenate([tail_ref[...], cur], axis=0)
    w = cw_ref[...]
    y = cb_ref[...]
    for k in range(SSD_CONV):
        y = y + w[k:k + 1, :] * xcat[5 + k:5 + k + L, :]
    tail_ref[...] = cur[L - 8:L, :]
    xbc = _silu(y)
    xs = xbc[:, :BRANCH_WIDTH]
    bm = xbc[:, BRANCH_WIDTH:BRANCH_WIDTH + SSD_GROUPS * SSD_STATE]
    cm = xbc[:, BRANCH_WIDTH + SSD_GROUPS * SSD_STATE:]

    lane = lax.broadcasted_iota(jnp.int32, (L, LANES), 1)
    head_lane = lane < SSD_HEADS
    dt = jnp.where(head_lane, _softplus(sm_ref[...] + dtb_ref[...]), 0.0)
    la = dt * (-jnp.exp(alog_ref[...]))
    causal = _tril(L)
    acum = _dot_hi(causal.astype(F32), la)
    acum_t = acum.T
    a_last = acum[L - 1:L, :]
    e = e_ref[...]
    dt_e = _dot_hi(dt, e)
    eac_e = jnp.exp(_dot_hi(acum, e))
    dte_e = jnp.exp(_dot_hi(a_last - acum, e))
    xdt = xs * dt_e
    xw = (xdt * dte_e).astype(BF16)
    xdt_b = xdt.astype(BF16)

    yoff = []
    for g in range(SSD_GROUPS):
        bg = bm[:, g * SSD_STATE:(g + 1) * SSD_STATE].astype(BF16)
        cg = cm[:, g * SSD_STATE:(g + 1) * SSD_STATE].astype(BF16)
        cb = _dot_nt(cg, bg)
        st = state_ref[:, g * gw:(g + 1) * gw]
        yoff.append(_dot(cg, st.astype(BF16)) * eac_e[:, g * gw:(g + 1) * gw])
        state_ref[:, g * gw:(g + 1) * gw] = (
            st * eac_e[L - 1:L, g * gw:(g + 1) * gw] + _dot_tn(bg, xw[:, g * gw:(g + 1) * gw]))
        for r in range(hpg):
            h = g * hpg + r
            seg = acum[:, h:h + 1] - acum_t[h:h + 1, :]
            dec = jnp.where(causal, jnp.exp(jnp.minimum(seg, 0.0)), 0.0)
            m = (cb * dec).astype(BF16)
            y_ref[:, h * SSD_HEAD_DIM:(h + 1) * SSD_HEAD_DIM] = _dot(
                m, xdt_b[:, h * SSD_HEAD_DIM:(h + 1) * SSD_HEAD_DIM])

    yt = y_ref[...] + jnp.concatenate(yoff, axis=1) + xs * dsk_ref[...]
    yt = yt * _silu(z_ref[...].astype(F32))
    nw = nw_ref[...]
    for g in range(SSD_GROUPS):
        yg = yt[:, g * gw:(g + 1) * gw]
        ms = jnp.mean(yg * yg, axis=-1, keepdims=True)
        o_ref[:, g * gw:(g + 1) * gw] = (
            yg * lax.rsqrt(ms + EPS) * nw[:, g * gw:(g + 1) * gw]).astype(BF16)


def _ssd(proj, small, conv_w, conv_b, dt_bias, a_log, d_skip, ssm_norm, bsz, s):
    L = SSD_CHUNK
    nc = s // L
    t = bsz * s
    pad = lambda v: jnp.zeros((1, LANES), F32).at[0, :SSD_HEADS].set(v.astype(F32))
    expand = jnp.zeros((LANES, BRANCH_WIDTH), F32).at[:SSD_HEADS].set(
        jnp.repeat(jnp.eye(SSD_HEADS, dtype=F32), SSD_HEAD_DIM, axis=1))
    dsk = jnp.repeat(d_skip.astype(F32), SSD_HEAD_DIM).reshape(1, BRANCH_WIDTH)
    row = lambda b, c: (b * nc + c, 0)
    const = lambda b, c: (0, 0)
    return pl.pallas_call(
        _ssd_kernel,
        grid=(bsz, nc),
        in_specs=[
            pl.BlockSpec((L, BRANCH_WIDTH), lambda b, c: (b * nc + c, _OFF["m_z"] // BRANCH_WIDTH)),
            pl.BlockSpec((L, SSD_XBC), lambda b, c: (b * nc + c, _OFF["m_xbc"] // SSD_XBC)),
            pl.BlockSpec((L, LANES), row),
            pl.BlockSpec((SSD_CONV, SSD_XBC), const),
            pl.BlockSpec((1, SSD_XBC), const),
            pl.BlockSpec((1, LANES), const),
            pl.BlockSpec((1, LANES), const),
            pl.BlockSpec((LANES, BRANCH_WIDTH), const),
            pl.BlockSpec((1, BRANCH_WIDTH), const),
            pl.BlockSpec((1, BRANCH_WIDTH), const),
        ],
        out_specs=pl.BlockSpec((L, BRANCH_WIDTH), row),
        out_shape=jax.ShapeDtypeStruct((t, BRANCH_WIDTH), BF16),
        scratch_shapes=[
            pltpu.VMEM((8, SSD_XBC), F32),
            pltpu.VMEM((SSD_STATE, BRANCH_WIDTH), F32),
            pltpu.VMEM((L, BRANCH_WIDTH), F32),
        ],
        compiler_params=_cparams(("parallel", "arbitrary")),
        name="ssd",
    )(proj, proj, small, conv_w.astype(F32), conv_b.astype(F32).reshape(1, SSD_XBC),
      pad(dt_bias), pad(a_log), expand, dsk, ssm_norm.astype(F32).reshape(1, BRANCH_WIDTH))


NEG_BIG = -1e30


def _flash_kernel(*refs, nsub, dk, dv, tq, has_bias, q_scale, head_lane0):
    if has_bias:
        q_ref, k_ref, v_ref, fq_ref, fk_ref, g_ref, o_ref = refs
    else:
        q_ref, k_ref, v_ref, g_ref, o_ref = refs
    hp = pl.program_id(1)
    qi = pl.program_id(2)

    qs, fcol = [], []
    for s in range(nsub):
        q = q_ref[:, s * dk:(s + 1) * dk]
        if q_scale != 1.0:
            q = (q.astype(F32) * q_scale).astype(BF16)
        qs.append(q)
        if has_bias:
            lane = lax.broadcasted_iota(jnp.int32, (tq, LANES), 1)
            pick = lane == head_lane0 + hp * nsub + s
            fcol.append(jnp.sum(jnp.where(pick, fq_ref[...], 0.0), axis=1, keepdims=True))

    def block(kstart, carry, masked):
        kb = k_ref[pl.ds(kstart, tq), :]
        vb = v_ref[pl.ds(kstart, tq), :]
        out = []
        for s in range(nsub):
            m, l, acc = carry[s]
            sc = _dot_nt(qs[s], kb[:, s * dk:(s + 1) * dk])
            if has_bias:
                sc = sc + fcol[s] - fk_ref[0, 0, s:s + 1, pl.ds(kstart, tq)]
            if masked:
                sc = jnp.where(_tril(tq), sc, NEG_BIG)
            m_new = jnp.maximum(m, jnp.max(sc, axis=1, keepdims=True))
            p = jnp.exp(sc - m_new)
            alpha = jnp.exp(m - m_new)
            l = alpha * l + jnp.sum(p, axis=1, keepdims=True)
            acc = alpha * acc + _dot(p.astype(BF16), vb[:, s * dv:(s + 1) * dv])
            out.append((m_new, l, acc))
        return tuple(out)

    init = tuple((jnp.full((tq, 1), NEG_BIG, F32), jnp.zeros((tq, 1), F32),
                  jnp.zeros((tq, dv), F32)) for _ in range(nsub))
    carry = lax.fori_loop(
        0, qi, lambda j, c: block(pl.multiple_of(j * tq, tq), c, False), init)
    carry = block(pl.multiple_of(qi * tq, tq), carry, True)
    for s in range(nsub):
        _, l, acc = carry[s]
        gate = g_ref[:, s * dv:(s + 1) * dv].astype(F32)
        o_ref[:, s * dv:(s + 1) * dv] = (acc / l * _silu(gate)).astype(BF16)


def _flash(q_arr, q_col0, k_arr, k_col0, v_arr, v_col0, gate_arr, gate_col0, bsz, s, tq,
           nsub, dk, dv, q_scale, bias=None):
    nq = s // tq
    nh = BRANCH_WIDTH // (nsub * dv)
    qmap = lambda c0: (lambda b, h, i: (b * nq + i, c0 + h))
    kvmap = lambda c0: (lambda b, h, i: (b, c0 + h))
    in_specs = [
        pl.BlockSpec((tq, nsub * dk), qmap(q_col0)),
        pl.BlockSpec((s, nsub * dk), kvmap(k_col0)),
        pl.BlockSpec((s, nsub * dv), kvmap(v_col0)),
    ]
    args = [q_arr, k_arr, v_arr]
    if bias is not None:
        fcum, fcum_t = bias
        in_specs += [
            pl.BlockSpec((tq, LANES), lambda b, h, i: (b * nq + i, 0)),
            pl.BlockSpec((1, 1, nsub, s), lambda b, h, i: (b, h, 0, 0)),
        ]
        args += [fcum, fcum_t]
    in_specs.append(pl.BlockSpec((tq, nsub * dv), qmap(gate_col0)))
    args.append(gate_arr)
    return pl.pallas_call(
        functools.partial(_flash_kernel, nsub=nsub, dk=dk, dv=dv, tq=tq,
                          has_bias=bias is not None, q_scale=q_scale, head_lane0=SM_FF),
        grid=(bsz, nh, nq),
        in_specs=in_specs,
        out_specs=pl.BlockSpec((tq, nsub * dv), lambda b, h, i: (b * nq + i, h)),
        out_shape=jax.ShapeDtypeStruct((bsz * s, BRANCH_WIDTH), BF16),
        compiler_params=_cparams(("parallel", "parallel", "arbitrary")),
        name="flash_fox" if bias is not None else "flash_mla",
    )(*args)


def _gla_kernel(q_ref, k_ref, v_ref, g_ref, sm_ref, w2_ref, b2_ref, nw_ref, o_ref, st_ref):
    C = GLA_CHUNK

    @pl.when(pl.program_id(2) == 0)
    def _():
        st_ref[...] = jnp.zeros_like(st_ref)

    la = _log_sigmoid(_dot_hi(sm_ref[...], w2_ref[0]) + b2_ref[0]) * (1.0 / GLA_GATE_TAU)
    bcum = _dot_hi(_tril(C).astype(F32), la)
    q = q_ref[...].astype(F32) * (GLA_KEY_DIM ** -0.5)
    k = k_ref[...].astype(F32)
    v = v_ref[...]
    st = st_ref[...]

    o = _dot_nt((q * jnp.exp(bcum)).astype(BF16), st.astype(BF16))

    rowi = lax.broadcasted_iota(jnp.int32, (C, C), 0)
    coli = lax.broadcasted_iota(jnp.int32, (C, C), 1)
    dist = rowi - coli
    attn = jnp.zeros((C, C), F32)
    for d in range(GLA_BAND):
        kr = k if d == 0 else pltpu.roll(k, d, 0)
        br = bcum if d == 0 else pltpu.roll(bcum, d, 0)
        term = q * kr * jnp.exp(jnp.minimum(bcum - br, 0.0))
        dd = jnp.sum(term, axis=-1, keepdims=True)
        attn = attn + jnp.where(dist == d, dd, 0.0)
    far = [jnp.zeros((GLA_BAND, C), F32)]
    for i0 in range(GLA_BAND, C, GLA_BAND):
        bref = bcum[i0 - 1:i0, :]
        qi = q[i0:i0 + GLA_BAND] * jnp.exp(bcum[i0:i0 + GLA_BAND] - bref)
        kj = k * jnp.exp(jnp.minimum(bref - bcum, 0.0))
        far.append(_dot_nt(qi.astype(BF16), kj.astype(BF16)))
    attn = attn + jnp.where(dist >= GLA_BAND, jnp.concatenate(far, axis=0), 0.0)
    o = o + _dot(attn.astype(BF16), v)

    b_last = bcum[C - 1:C, :]
    kd = (k * jnp.exp(b_last - bcum)).astype(BF16)
    st_ref[...] = st * jnp.exp(b_last) + _dot_tn(v, kd)

    ms = jnp.mean(o * o, axis=-1, keepdims=True)
    on = o * lax.rsqrt(ms + EPS) * nw_ref[...]
    o_ref[...] = (on * _silu(g_ref[...].astype(F32))).astype(BF16)


def _gla(proj, small, gla_w2, gla_b, gla_norm, bsz, s):
    C = GLA_CHUNK
    nc = s // C
    t = bsz * s
    w2 = jnp.zeros((GLA_HEADS, LANES, GLA_KEY_DIM), F32).at[:, SM_LR:SM_LR + GLA_GATE_RANK].set(
        gla_w2.astype(F32).reshape(GLA_GATE_RANK, GLA_HEADS, GLA_KEY_DIM).transpose(1, 0, 2))
    b2 = gla_b.astype(F32).reshape(GLA_HEADS, 1, GLA_KEY_DIM)
    cmap = lambda name, w: (lambda b, h, c: (b * nc + c, _OFF[name] // w + h))
    return pl.pallas_call(
        _gla_kernel,
        grid=(bsz, GLA_HEADS, nc),
        in_specs=[
            pl.BlockSpec((C, GLA_KEY_DIM), cmap("g_q", GLA_KEY_DIM)),
            pl.BlockSpec((C, GLA_KEY_DIM), cmap("g_k", GLA_KEY_DIM)),
            pl.BlockSpec((C, GLA_VAL_DIM), cmap("g_v", GLA_VAL_DIM)),
            pl.BlockSpec((C, GLA_VAL_DIM), cmap("g_gate", GLA_VAL_DIM)),
            pl.BlockSpec((C, LANES), lambda b, h, c: (b * nc + c, 0)),
            pl.BlockSpec((1, LANES, GLA_KEY_DIM), lambda b, h, c: (h, 0, 0)),
            pl.BlockSpec((1, 1, GLA_KEY_DIM), lambda b, h, c: (h, 0, 0)),
            pl.BlockSpec((1, GLA_VAL_DIM), lambda b, h, c: (0, 0)),
        ],
        out_specs=pl.BlockSpec((C, GLA_VAL_DIM), lambda b, h, c: (b * nc + c, h)),
        out_shape=jax.ShapeDtypeStruct((t, BRANCH_WIDTH), BF16),
        scratch_shapes=[pltpu.VMEM((GLA_VAL_DIM, GLA_KEY_DIM), F32)],
        compiler_params=_cparams(("parallel", "parallel", "arbitrary")),
        name="gla",
    )(proj, proj, proj, proj, small, w2, b2, gla_norm.astype(F32).reshape(1, GLA_VAL_DIM))


def _mla_prep_kernel(cq_ref, ckv_ref, kr_ref, krs_ref, pos_ref, invf_ref, sign_ref, qn_ref,
                     kvn_ref, wqn_ref, wqr_ref, wqs_ref, wk_ref, wv_ref, q_ref, k_ref, v_ref):
    def norm(ref, w_ref):
        c = ref[...].astype(F32)
        ms = jnp.mean(c * c, axis=-1, keepdims=True)
        return (c * lax.rsqrt(ms + EPS) * w_ref[...]).astype(BF16)

    cq = norm(cq_ref, qn_ref)
    ckv = norm(ckv_ref, kvn_ref)
    ang = pos_ref[...] * invf_ref[...]
    cos = jnp.cos(ang)
    sin = jnp.sin(ang) * sign_ref[...]
    scale = (MLA_NOPE + MLA_ROPE) ** -0.5

    qn = _dot(cq, wqn_ref[...]) * scale
    qr = _dot(cq, wqr_ref[...])
    qsw = _dot(cq, wqs_ref[...])
    kn = _dot(ckv, wk_ref[...])
    v_ref[...] = _dot(ckv, wv_ref[...]).astype(BF16)
    kr = (kr_ref[...] * cos + krs_ref[...] * sin).astype(BF16)
    for h in range(MLA_HEADS):
        lo, hi = h * LANES, (h + 1) * LANES
        base = h * MLA_QK_PAD
        q_ref[:, base:base + LANES] = qn[:, lo:hi].astype(BF16)
        q_ref[:, base + LANES:base + 2 * LANES] = (
            (qr[:, lo:hi] * cos + qsw[:, lo:hi] * sin) * scale).astype(BF16)
        k_ref[:, base:base + LANES] = kn[:, lo:hi].astype(BF16)
        k_ref[:, base + LANES:base + 2 * LANES] = kr


def _mla_prep(proj, small, posf, q_norm, kv_norm, w_uq, w_ukv, tm):
    t = proj.shape[0]
    half = MLA_ROPE // 2
    wq = w_uq.reshape(MLA_Q_RANK, MLA_HEADS, MLA_NOPE + MLA_ROPE)
    zpad = jnp.zeros((MLA_Q_RANK, MLA_HEADS, LANES - MLA_ROPE), w_uq.dtype)
    flat = lambda a: a.reshape(MLA_Q_RANK, MLA_HEADS * LANES).astype(BF16)
    wqn = flat(wq[:, :, :MLA_NOPE])
    wqr = flat(jnp.concatenate([wq[:, :, MLA_NOPE:], zpad], axis=2))
    wqs = flat(jnp.concatenate([wq[:, :, MLA_NOPE + half:], wq[:, :, MLA_NOPE:MLA_NOPE + half],
                                zpad], axis=2))
    wkv = w_ukv.reshape(MLA_KV_RANK, MLA_HEADS, MLA_NOPE + MLA_V)
    wk = flat(wkv[:, :, :MLA_NOPE])
    wv = flat(wkv[:, :, MLA_NOPE:])
    inv_freq = 1.0 / (ROPE_BASE ** (jnp.arange(half, dtype=F32) * 2.0 / MLA_ROPE))
    zero = jnp.zeros((LANES - MLA_ROPE,), F32)
    invf = jnp.concatenate([inv_freq, inv_freq, zero]).reshape(1, LANES)
    sign = jnp.concatenate([-jnp.ones((half,), F32), jnp.ones((half,), F32), zero]).reshape(1, LANES)
    wide = MLA_HEADS * MLA_QK_PAD
    row = lambda c: (lambda i: (i, c))
    const = lambda i: (0, 0)
    wspec = pl.BlockSpec((MLA_Q_RANK, MLA_HEADS * LANES), const)
    return pl.pallas_call(
        _mla_prep_kernel,
        grid=(t // tm,),
        in_specs=[
            pl.BlockSpec((tm, MLA_Q_RANK), row(_OFF["l_cq"] // MLA_Q_RANK)),
            pl.BlockSpec((tm, MLA_KV_RANK), row(_OFF["l_ckv"] // MLA_KV_RANK)),
            pl.BlockSpec((tm, LANES), row(1)),
            pl.BlockSpec((tm, LANES), row(2)),
            pl.BlockSpec((tm, 1), row(0)),
            pl.BlockSpec((1, LANES), const),
            pl.BlockSpec((1, LANES), const),
            pl.BlockSpec((1, MLA_Q_RANK), const),
            pl.BlockSpec((1, MLA_KV_RANK), const),
            wspec, wspec, wspec, wspec, wspec,
        ],
        out_specs=[
            pl.BlockSpec((tm, wide), row(0)),
            pl.BlockSpec((tm, wide), row(0)),
            pl.BlockSpec((tm, BRANCH_WIDTH), row(0)),
        ],
        out_shape=[
            jax.ShapeDtypeStruct((t, wide), BF16),
            jax.ShapeDtypeStruct((t, wide), BF16),
            jax.ShapeDtypeStruct((t, BRANCH_WIDTH), BF16),
        ],
        compiler_params=_cparams(("parallel",)),
        name="mla_prep",
    )(proj, proj, small, small, posf, invf, sign, q_norm.astype(F32).reshape(1, MLA_Q_RANK),
      kv_norm.astype(F32).reshape(1, MLA_KV_RANK), wqn, wqr, wqs, wk, wv)


def _merge_kernel(a_ref, b_ref, c_ref, d_ref, g0_ref, g1_ref, g2_ref, g3_ref, w_ref, o_ref):
    acc = None
    for br, (x_ref, g_ref) in enumerate(
            ((a_ref, g0_ref), (b_ref, g1_ref), (c_ref, g2_ref), (d_ref, g3_ref))):
        gate = 1.0 / (1.0 + jnp.exp(-g_ref[...].astype(F32)))
        term = gate * _dot(x_ref[...], w_ref[br])
        acc = term if acc is None else acc + term
    o_ref[...] = acc.astype(BF16)


def _merge(outs, proj, w_branch, tm, tn):
    t = proj.shape[0]
    nn = D_MODEL // tn
    xspec = pl.BlockSpec((tm, BRANCH_WIDTH), lambda i, n: (i, 0))
    gspec = lambda br: pl.BlockSpec((tm, tn), lambda i, n: (i, _OFF["merge"] // tn + br * nn + n))
    return pl.pallas_call(
        _merge_kernel,
        grid=(t // tm, nn),
        in_specs=[xspec] * 4 + [gspec(br) for br in range(N_BRANCHES)] + [
            pl.BlockSpec((N_BRANCHES, BRANCH_WIDTH, tn), lambda i, n: (0, 0, n))],
        out_specs=pl.BlockSpec((tm, tn), lambda i, n: (i, n)),
        out_shape=jax.ShapeDtypeStruct((t, D_MODEL), BF16),
        compiler_params=_cparams(("parallel", "arbitrary")),
        name="merge",
    )(*outs, proj, proj, proj, proj, w_branch.astype(BF16))


def _out_proj_kernel(m_ref, x_ref, w_ref, nw_ref, o_ref):
    y = _dot(m_ref[...], w_ref[...])
    ms = jnp.mean(y * y, axis=-1, keepdims=True)
    o_ref[...] = x_ref[...] + y * lax.rsqrt(ms + EPS) * nw_ref[...]


def _out_proj(mixed, x2d, w_out, post_norm, tm):
    t = x2d.shape[0]
    return pl.pallas_call(
        _out_proj_kernel,
        grid=(t // tm,),
        in_specs=[
            pl.BlockSpec((tm, D_MODEL), lambda i: (i, 0)),
            pl.BlockSpec((tm, D_MODEL), lambda i: (i, 0)),
            pl.BlockSpec((D_MODEL, D_MODEL), lambda i: (0, 0)),
            pl.BlockSpec((1, D_MODEL), lambda i: (0, 0)),
        ],
        out_specs=pl.BlockSpec((tm, D_MODEL), lambda i: (i, 0)),
        out_shape=jax.ShapeDtypeStruct((t, D_MODEL), F32),
        compiler_params=_cparams(("parallel",)),
        name="out_proj",
    )(mixed, x2d, w_out.astype(BF16), post_norm.astype(F32).reshape(1, D_MODEL))


def _pack_w_in(w_in):
    bounds = np.cumsum((0,) + _IN_SIZES)
    seg = {n: w_in[:, bounds[i]:bounds[i + 1]] for i, n in enumerate(_IN_NAMES)}
    big = jnp.concatenate([seg[n] for n in _BIG_ORDER], axis=1).astype(BF16)
    half = MLA_ROPE // 2
    kr = seg["l_kr"]
    z = lambda n: jnp.zeros((D_MODEL, n), w_in.dtype)
    small = jnp.concatenate([
        seg["m_dt"], seg["f_f"], seg["g_lr"], z(LANES - 48),
        kr, z(LANES - MLA_ROPE),
        kr[:, half:], kr[:, :half], z(LANES - MLA_ROPE)], axis=1).astype(BF16)
    return big, small


def _tile(n, want):
    return want if n % want == 0 else n


def _layer(x2d, posf, bsz, s, pre_norm, post_norm, w_in, conv_w, conv_b, dt_bias, a_log, d_skip,
           ssm_norm, fgate_b, gla_w2, gla_b, gla_norm, q_norm, kv_norm, w_uq, w_ukv, w_branch,
           w_out):
    t = bsz * s
    w_big, w_small = _pack_w_in(w_in)
    proj, small = _in_proj(x2d, pre_norm.astype(F32), w_big, w_small, _tile(t, 1024), 768)

    out_a = _ssd(proj, small, conv_w, conv_b, dt_bias, a_log, d_skip, ssm_norm, bsz, s)

    fcum, fcum_t = _fcum(small, fgate_b, bsz, s, _tile(s, 512))
    fk = fcum_t[:, SM_FF:SM_FF + FOX_HEADS, :].reshape(bsz, FOX_HEADS // 2, 2, s)
    tq = _tile(s, 256)
    pair = 2 * FOX_HEAD_DIM
    out_b = _flash(proj, _OFF["f_q"] // pair, proj, _OFF["f_k"] // pair, proj, _OFF["f_v"] // pair,
                   proj, _OFF["f_gate"] // pair, bsz, s, tq, nsub=2, dk=FOX_HEAD_DIM,
                   dv=FOX_HEAD_DIM, q_scale=FOX_HEAD_DIM ** -0.5, bias=(fcum, fk))

    out_c = _gla(proj, small, gla_w2, gla_b, gla_norm, bsz, s)

    q_full, k_full, v_mla = _mla_prep(proj, small, posf, q_norm, kv_norm, w_uq, w_ukv,
                                      _tile(t, 512))
    out_d = _flash(q_full, 0, k_full, 0, v_mla, 0, proj, _OFF["l_gate"] // MLA_V, bsz, s, tq,
                   nsub=1, dk=MLA_QK_PAD, dv=MLA_V, q_scale=1.0)

    mixed = _merge((out_a, out_b, out_c, out_d), proj, w_branch, _tile(t, 1024), 512)
    return _out_proj(mixed, x2d, w_out, post_norm, _tile(t, 512))


def kernel(x, positions, pre_norm, post_norm, w_in, conv_w, conv_b, dt_bias, a_log, d_skip,
           ssm_norm, fgate_b, gla_w2, gla_b, gla_norm, q_norm, kv_norm, w_uq, w_ukv, w_branch,
           w_out):
    bsz, s, _ = x.shape
    x2d = x.reshape(bsz * s, D_MODEL)
    posf = positions.astype(F32).reshape(bsz * s, 1)
    for l in range(pre_norm.shape[0]):
        x2d = _layer(x2d, posf, bsz, s, pre_norm[l], post_norm[l], w_in[l], conv_w[l], conv_b[l],
                     dt_bias[l], a_log[l], d_skip[l], ssm_norm[l], fgate_b[l], gla_w2[l], gla_b[l],
                     gla_norm[l], q_norm[l], kv_norm[l], w_uq[l], w_ukv[l], w_branch[l], w_out[l])
    return x2d.reshape(bsz, s, D_MODEL)
```

```python
import functools

import jax
import jax.numpy as jnp
import numpy as np
from jax import lax
from jax.experimental import pallas as pl
from jax.experimental.pallas import tpu as pltpu

F32 = jnp.float32
BF16 = jnp.bfloat16
HIGHEST = lax.Precision.HIGHEST

D_MODEL = 2048
BRANCH_WIDTH = 1024
N_BRANCHES = 4
EPS = 1e-6

SSD_HEAD_DIM = 64
SSD_HEADS = 16
SSD_GROUPS = 2
SSD_STATE = 128
SSD_CONV = 4
SSD_CHUNK = 128
SSD_XBC = 1536

FOX_HEAD_DIM = 64
FOX_HEADS = 16

GLA_HEADS = 4
GLA_KEY_DIM = 128
GLA_VAL_DIM = 256
GLA_GATE_RANK = 16
GLA_GATE_TAU = 16.0
GLA_CHUNK = 64
GLA_BAND = 16

MLA_HEADS = 8
MLA_NOPE = 128
MLA_ROPE = 64
MLA_V = 128
MLA_Q_RANK = 512
MLA_KV_RANK = 512
MLA_QK_PAD = 256
ROPE_BASE = 10000.0

LANES = 128
VMEM_LIMIT = 56 * 1024 * 1024

_IN_NAMES = ("m_z", "m_xbc", "m_dt", "f_q", "f_k", "f_v", "f_f", "f_gate",
             "g_q", "g_k", "g_v", "g_lr", "g_gate", "l_cq", "l_ckv", "l_kr", "l_gate", "merge")
_IN_SIZES = (1024, 1536, 16, 1024, 1024, 1024, 16, 1024,
             512, 512, 1024, 16, 1024, 512, 512, 64, 1024, 8192)
_BIG_ORDER = ("merge", "m_z", "f_q", "f_k", "f_v", "f_gate", "g_v", "g_gate", "l_gate",
              "g_q", "g_k", "l_cq", "l_ckv", "m_xbc")
_SIZE = dict(zip(_IN_NAMES, _IN_SIZES))
_OFF = {}
_o = 0
for _n in _BIG_ORDER:
    _OFF[_n] = _o
    _o += _SIZE[_n]
N_BIG = _o
N_SMALL = 3 * LANES
SM_DT, SM_FF, SM_LR = 0, 16, 32


def _cparams(sem):
    return pltpu.CompilerParams(dimension_semantics=sem, vmem_limit_bytes=VMEM_LIMIT)


def _silu(v):
    return v / (1.0 + jnp.exp(-v))


def _softplus(v):
    return jnp.maximum(v, 0.0) + jnp.log1p(jnp.exp(-jnp.abs(v)))


def _log_sigmoid(v):
    return jnp.minimum(v, 0.0) - jnp.log1p(jnp.exp(-jnp.abs(v)))


def _tril(n):
    r = lax.broadcasted_iota(jnp.int32, (n, n), 0)
    c = lax.broadcasted_iota(jnp.int32, (n, n), 1)
    return r >= c


def _dot(a, b):
    return jnp.dot(a, b, preferred_element_type=F32)


def _dot_nt(a, b):
    return lax.dot_general(a, b, (((1,), (1,)), ((), ())), preferred_element_type=F32)


def _dot_tn(a, b):
    return lax.dot_general(a, b, (((0,), (0,)), ((), ())), preferred_element_type=F32)


def _dot_hi(a, b):
    return jnp.dot(a, b, precision=HIGHEST, preferred_element_type=F32)


def _in_proj_kernel(x_ref, pn_ref, w_ref, ws_ref, o_ref, os_ref, h_ref):
    @pl.when(pl.program_id(1) == 0)
    def _():
        x = x_ref[...]
        ms = jnp.mean(x * x, axis=-1, keepdims=True)
        hb = (x * lax.rsqrt(ms + EPS) * pn_ref[...]).astype(BF16)
        h_ref[...] = hb
        os_ref[...] = _dot(hb, ws_ref[...])

    o_ref[...] = _dot(h_ref[...], w_ref[...]).astype(BF16)


def _in_proj(x2d, pre_norm, w_big, w_small, tm, tn):
    t = x2d.shape[0]
    return pl.pallas_call(
        _in_proj_kernel,
        grid=(t // tm, N_BIG // tn),
        in_specs=[
            pl.BlockSpec((tm, D_MODEL), lambda i, j: (i, 0)),
            pl.BlockSpec((1, D_MODEL), lambda i, j: (0, 0)),
            pl.BlockSpec((D_MODEL, tn), lambda i, j: (0, j)),
            pl.BlockSpec((D_MODEL, N_SMALL), lambda i, j: (0, 0)),
        ],
        out_specs=[
            pl.BlockSpec((tm, tn), lambda i, j: (i, j)),
            pl.BlockSpec((tm, N_SMALL), lambda i, j: (i, 0)),
        ],
        out_shape=[
            jax.ShapeDtypeStruct((t, N_BIG), BF16),
            jax.ShapeDtypeStruct((t, N_SMALL), F32),
        ],
        scratch_shapes=[pltpu.VMEM((tm, D_MODEL), BF16)],
        compiler_params=_cparams(("parallel", "arbitrary")),
        name="in_proj",
    )(x2d, pre_norm.reshape(1, D_MODEL), w_big, w_small)


def _fcum_kernel(s_ref, fb_ref, o_ref, carry_ref, *, tc):
    @pl.when(pl.program_id(1) == 0)
    def _():
        carry_ref[...] = jnp.zeros_like(carry_ref)

    lf = _log_sigmoid(s_ref[...] + fb_ref[...])
    cum = _dot_hi(_tril(tc).astype(F32), lf) + carry_ref[...]
    o_ref[...] = cum
    carry_ref[...] = cum[tc - 1:tc, :]


def _fcum(small, fgate_b, bsz, s, tc):
    fb = jnp.zeros((1, LANES), F32).at[0, SM_FF:SM_FF + FOX_HEADS].set(fgate_b.astype(F32))
    nj = s // tc
    return pl.pallas_call(
        functools.partial(_fcum_kernel, tc=tc),
        grid=(bsz, nj),
        in_specs=[
            pl.BlockSpec((tc, LANES), lambda b, j: (b * nj + j, 0)),
            pl.BlockSpec((1, LANES), lambda b, j: (0, 0)),
        ],
        out_specs=pl.BlockSpec((tc, LANES), lambda b, j: (b * nj + j, 0)),
        out_shape=jax.ShapeDtypeStruct((bsz * s, LANES), F32),
        scratch_shapes=[pltpu.VMEM((1, LANES), F32)],
        compiler_params=_cparams(("parallel", "arbitrary")),
        name="fcum",
    )(small, fb)


def _ssd_kernel(z_ref, xbc_ref, sm_ref, cw_ref, cb_ref, dtb_ref, alog_ref, e_ref, dsk_ref,
                nw_ref, o_ref, tail_ref, state_ref, y_ref):
    L = SSD_CHUNK
    gw = BRANCH_WIDTH // SSD_GROUPS
    hpg = SSD_HEADS // SSD_GROUPS

    @pl.when(pl.program_id(1) == 0)
    def _():
        tail_ref[...] = jnp.zeros_like(tail_ref)
        state_ref[...] = jnp.zeros_like(state_ref)

    cur = xbc_ref[...].astype(F32)
    xcat = jnp.concatenate([tail_ref[...], cur], axis=0)
    w = cw_ref[...]
    y = cb_ref[...]
    for k in range(SSD_CONV):
        y = y + w[k:k + 1, :] * xcat[5 + k:5 + k + L, :]
    tail_ref[...] = cur[L - 8:L, :]
    xbc = _silu(y)
    xs = xbc[:, :BRANCH_WIDTH]
    bm = xbc[:, BRANCH_WIDTH:BRANCH_WIDTH + SSD_GROUPS * SSD_STATE]
    cm = xbc[:, BRANCH_WIDTH + SSD_GROUPS * SSD_STATE:]

    lane = lax.broadcasted_iota(jnp.int32, (L, LANES), 1)
    head_lane = lane < SSD_HEADS
    dt = jnp.where(head_lane, _softplus(sm_ref[...] + dtb_ref[...]), 0.0)
    la = dt * (-jnp.exp(alog_ref[...]))
    causal = _tril(L)
    acum = _dot_hi(causal.astype(F32), la)
    acum_t = acum.T
    a_last = acum[L - 1:L, :]
    e = e_ref[...]
    dt_e = _dot_hi(dt, e)
    eac_e = jnp.exp(_dot_hi(acum, e))
    dte_e = jnp.exp(_dot_hi(a_last - acum, e))
    xdt = xs * dt_e
    xw = (xdt * dte_e).astype(BF16)
    xdt_b = xdt.astype(BF16)

    yoff = []
    for g in range(SSD_GROUPS):
        bg = bm[:, g * SSD_STATE:(g + 1) * SSD_STATE].astype(BF16)
        cg = cm[:, g * SSD_STATE:(g + 1) * SSD_STATE].astype(BF16)
        cb = _dot_nt(cg, bg)
        st = state_ref[:, g * gw:(g + 1) * gw]
        yoff.append(_dot(cg, st.astype(BF16)) * eac_e[:, g * gw:(g + 1) * gw])
        state_ref[:, g * gw:(g + 1) * gw] = (
            st * eac_e[L - 1:L, g * gw:(g + 1) * gw] + _dot_tn(bg, xw[:, g * gw:(g + 1) * gw]))
        for r in range(hpg):
            h = g * hpg + r
            seg = acum[:, h:h + 1] - acum_t[h:h + 1, :]
            dec = jnp.where(causal, jnp.exp(jnp.minimum(seg, 0.0)), 0.0)
            m = (cb * dec).astype(BF16)
            y_ref[:, h * SSD_HEAD_DIM:(h + 1) * SSD_HEAD_DIM] = _dot(
                m, xdt_b[:, h * SSD_HEAD_DIM:(h + 1) * SSD_HEAD_DIM])

    yt = y_ref[...] + jnp.concatenate(yoff, axis=1) + xs * dsk_ref[...]
    yt = yt * _silu(z_ref[...].astype(F32))
    nw = nw_ref[...]
    for g in range(SSD_GROUPS):
        yg = yt[:, g * gw:(g + 1) * gw]
        ms = jnp.mean(yg * yg, axis=-1, keepdims=True)
        o_ref[:, g * gw:(g + 1) * gw] = (
            yg * lax.rsqrt(ms + EPS) * nw[:, g * gw:(g + 1) * gw]).astype(BF16)


def _ssd(proj, small, conv_w, conv_b, dt_bias, a_log, d_skip, ssm_norm, bsz, s):
    L = SSD_CHUNK
    nc = s // L
    t = bsz * s
    pad = lambda v: jnp.zeros((1, LANES), F32).at[0, :SSD_HEADS].set(v.astype(F32))
    expand = jnp.zeros((LANES, BRANCH_WIDTH), F32).at[:SSD_HEADS].set(
        jnp.repeat(jnp.eye(SSD_HEADS, dtype=F32), SSD_HEAD_DIM, axis=1))
    dsk = jnp.repeat(d_skip.astype(F32), SSD_HEAD_DIM).reshape(1, BRANCH_WIDTH)
    row = lambda b, c: (b * nc + c, 0)
    const = lambda b, c: (0, 0)
    return pl.pallas_call(
        _ssd_kernel,
        grid=(bsz, nc),
        in_specs=[
            pl.BlockSpec((L, BRANCH_WIDTH), lambda b, c: (b * nc + c, _OFF["m_z"] // BRANCH_WIDTH)),
            pl.BlockSpec((L, SSD_XBC), lambda b, c: (b * nc + c, _OFF["m_xbc"] // SSD_XBC)),
            pl.BlockSpec((L, LANES), row),
            pl.BlockSpec((SSD_CONV, SSD_XBC), const),
            pl.BlockSpec((1, SSD_XBC), const),
            pl.BlockSpec((1, LANES), const),
            pl.BlockSpec((1, LANES), const),
            pl.BlockSpec((LANES, BRANCH_WIDTH), const),
            pl.BlockSpec((1, BRANCH_WIDTH), const),
            pl.BlockSpec((1, BRANCH_WIDTH), const),
        ],
        out_specs=pl.BlockSpec((L, BRANCH_WIDTH), row),
        out_shape=jax.ShapeDtypeStruct((t, BRANCH_WIDTH), BF16),
        scratch_shapes=[
            pltpu.VMEM((8, SSD_XBC), F32),
            pltpu.VMEM((SSD_STATE, BRANCH_WIDTH), F32),
            pltpu.VMEM((L, BRANCH_WIDTH), F32),
        ],
        compiler_params=_cparams(("parallel", "arbitrary")),
        name="ssd",
    )(proj, proj, small, conv_w.astype(F32), conv_b.astype(F32).reshape(1, SSD_XBC),
      pad(dt_bias), pad(a_log), expand, dsk, ssm_norm.astype(F32).reshape(1, BRANCH_WIDTH))


NEG_BIG = -1e30
LOG2E = 1.4426950408889634


def _split3(v):
    hi = v.astype(BF16).astype(F32)
    mid = (v - hi).astype(BF16).astype(F32)
    lo = v - hi - mid
    return hi, mid, lo


def _flash_kernel(*refs, nsub, dk, dv, tq, tk, seq, has_bias, q_scale, head_lane0):
    if has_bias:
        q_ref, k_ref, v_ref, fq_ref, fs_ref, g_ref, o_ref, vt_ref, ka_ref = refs
    else:
        q_ref, k_ref, v_ref, g_ref, o_ref, vt_ref = refs
    hp = pl.program_id(1)
    qi = pl.program_id(2)
    ratio = tq // tk

    def with_bias_lanes(x, f_col, s, query_side):
        rows = x.shape[0]
        lane = lax.broadcasted_iota(jnp.int32, (rows, LANES), 1)
        own = (lane >= s * dk) & (lane < (s + 1) * dk)
        c0 = (1 - s) * dk
        hi, mid, lo = _split3(f_col * LOG2E)
        one = jnp.ones_like(f_col)
        pieces = (hi, mid, lo, one, one, one) if query_side else (one, one, one, -hi, -mid, -lo)
        extra = jnp.zeros((rows, LANES), F32)
        for j, pc in enumerate(pieces):
            extra = jnp.where(lane == c0 + j, pc, extra)
        return jnp.where(own, x, extra).astype(BF16)

    def pick_head(f, s):
        lane = lax.broadcasted_iota(jnp.int32, f.shape, 1)
        return jnp.sum(jnp.where(lane == head_lane0 + hp * nsub + s, f, 0.0), axis=1,
                       keepdims=True)

    @pl.when(qi == 0)
    def _():
        for c in range(seq // tk):
            rows = slice(c * tk, (c + 1) * tk)
            vt_ref[c] = v_ref[rows, :].astype(F32).T.astype(BF16)
            if has_bias:
                kf = k_ref[rows, :].astype(F32)
                f = fs_ref[rows, :]
                for s in range(nsub):
                    ka_ref[s, rows, :] = with_bias_lanes(kf, pick_head(f, s), s, False)

    if has_bias:
        qf = q_ref[...].astype(F32) * (q_scale * LOG2E)
        fq = fq_ref[...]
        qa = [with_bias_lanes(qf, pick_head(fq, s), s, True) for s in range(nsub)]
    else:
        qa = [q_ref[...]]

    def block(kj, carry, lane0, masked):
        w = tq - lane0
        out = []
        for s in range(nsub):
            m, l, acc = carry[s]
            rows = pl.ds(pl.multiple_of(kj * tk, tk), tk)
            kb = ka_ref[s, rows, :] if has_bias else k_ref[rows, :]
            st = _dot_nt(kb, qa[s][lane0:, :])
            if masked:
                key = lax.broadcasted_iota(jnp.int32, (tk, w), 0)
                qry = lax.broadcasted_iota(jnp.int32, (tk, w), 1)
                st = jnp.where(key <= qry, st, NEG_BIG)
            m_old = m[:, lane0:]
            m_new = jnp.maximum(m_old, jnp.max(st, axis=0, keepdims=True))
            p = jnp.exp2(st - m_new)
            alpha = jnp.exp2(m_old - m_new)
            l_new = alpha * l[:, lane0:] + jnp.sum(p, axis=0, keepdims=True)
            vt = vt_ref[kj, s * dv:(s + 1) * dv, :]
            acc_new = alpha * acc[:, lane0:] + _dot(vt, p.astype(BF16))
            if lane0:
                m_new = jnp.concatenate([m[:, :lane0], m_new], axis=1)
                l_new = jnp.concatenate([l[:, :lane0], l_new], axis=1)
                acc_new = jnp.concatenate([acc[:, :lane0], acc_new], axis=1)
            out.append((m_new, l_new, acc_new))
        return tuple(out)

    init = tuple((jnp.full((1, tq), NEG_BIG, F32), jnp.zeros((1, tq), F32),
                  jnp.zeros((dv, tq), F32)) for _ in range(nsub))
    carry = lax.fori_loop(0, qi * ratio, lambda j, c: block(j, c, 0, False), init)
    for d in range(ratio):
        carry = block(qi * ratio + d, carry, d * tk, True)
    ot = jnp.concatenate([acc / l for (_, l, acc) in carry], axis=0)
    o_ref[...] = (ot.T * _silu(g_ref[...].astype(F32))).astype(BF16)


def _flash(q_arr, q_col0, k_arr, k_col0, v_arr, v_col0, gate_arr, gate_col0, bsz, s, tq, tk,
           nsub, dk, dv, q_scale, fcum=None):
    nq = s // tq
    nh = BRANCH_WIDTH // (nsub * dv)
    qmap = lambda c0: (lambda b, h, i: (b * nq + i, c0 + h))
    kvmap = lambda c0: (lambda b, h, i: (b, c0 + h))
    in_specs = [
        pl.BlockSpec((tq, nsub * dk), qmap(q_col0)),
        pl.BlockSpec((s, nsub * dk), kvmap(k_col0)),
        pl.BlockSpec((s, nsub * dv), kvmap(v_col0)),
    ]
    args = [q_arr, k_arr, v_arr]
    scratch = [pltpu.VMEM((s // tk, nsub * dv, tk), BF16)]
    if fcum is not None:
        in_specs += [
            pl.BlockSpec((tq, LANES), lambda b, h, i: (b * nq + i, 0)),
            pl.BlockSpec((s, LANES), lambda b, h, i: (b, 0)),
        ]
        args += [fcum, fcum]
        scratch.append(pltpu.VMEM((nsub, s, LANES), BF16))
    in_specs.append(pl.BlockSpec((tq, nsub * dv), qmap(gate_col0)))
    args.append(gate_arr)
    return pl.pallas_call(
        functools.partial(_flash_kernel, nsub=nsub, dk=dk, dv=dv, tq=tq, tk=tk, seq=s,
                          has_bias=fcum is not None, q_scale=q_scale, head_lane0=SM_FF),
        grid=(bsz, nh, nq),
        in_specs=in_specs,
        out_specs=pl.BlockSpec((tq, nsub * dv), lambda b, h, i: (b * nq + i, h)),
        out_shape=jax.ShapeDtypeStruct((bsz * s, BRANCH_WIDTH), BF16),
        scratch_shapes=scratch,
        compiler_params=_cparams(("parallel", "parallel", "arbitrary")),
        name="flash_fox" if fcum is not None else "flash_mla",
    )(*args)


def _gla_kernel(q_ref, k_ref, v_ref, g_ref, sm_ref, w2_ref, b2_ref, nw_ref, o_ref, st_ref):
    C, H, DK, DV = GLA_CHUNK, GLA_HEADS, GLA_KEY_DIM, GLA_VAL_DIM

    @pl.when(pl.program_id(1) == 0)
    def _():
        st_ref[...] = jnp.zeros_like(st_ref)

    la = _log_sigmoid(_dot_hi(sm_ref[...], w2_ref[...]) + b2_ref[...]) * (1.0 / GLA_GATE_TAU)
    bcum = _dot_hi(_tril(C).astype(F32), la)
    q = q_ref[...].astype(F32) * (DK ** -0.5)
    k = k_ref[...].astype(F32)
    b_last = bcum[C - 1:C, :]
    qe = (q * jnp.exp(bcum)).astype(BF16)
    kd = (k * jnp.exp(b_last - bcum)).astype(BF16)
    decay_all = jnp.exp(b_last)

    rowi = lax.broadcasted_iota(jnp.int32, (C, C), 0)
    coli = lax.broadcasted_iota(jnp.int32, (C, C), 1)
    dist = rowi - coli
    attn = [jnp.zeros((C, C), F32) for _ in range(H)]
    for d in range(GLA_BAND):
        kr = k if d == 0 else pltpu.roll(k, d, 0)
        br = bcum if d == 0 else pltpu.roll(bcum, d, 0)
        term = q * kr * jnp.exp(jnp.minimum(bcum - br, 0.0))
        for h in range(H):
            dd = jnp.sum(term[:, h * DK:(h + 1) * DK], axis=-1, keepdims=True)
            attn[h] = attn[h] + jnp.where(dist == d, dd, 0.0)
    far = [[jnp.zeros((GLA_BAND, C), F32)] for _ in range(H)]
    for i0 in range(GLA_BAND, C, GLA_BAND):
        bref = bcum[i0 - 1:i0, :]
        qi = (q[i0:i0 + GLA_BAND] * jnp.exp(bcum[i0:i0 + GLA_BAND] - bref)).astype(BF16)
        kj = (k * jnp.exp(jnp.minimum(bref - bcum, 0.0))).astype(BF16)
        for h in range(H):
            far[h].append(_dot_nt(qi[:, h * DK:(h + 1) * DK], kj[:, h * DK:(h + 1) * DK]))

    nw = nw_ref[...]
    for h in range(H):
        ks, vs = slice(h * DK, (h + 1) * DK), slice(h * DV, (h + 1) * DV)
        a = attn[h] + jnp.where(dist >= GLA_BAND, jnp.concatenate(far[h], axis=0), 0.0)
        v = v_ref[:, vs]
        st = st_ref[h]
        o = _dot_nt(qe[:, ks], st.astype(BF16)) + _dot(a.astype(BF16), v)
        st_ref[h] = st * decay_all[:, ks] + _dot_tn(v, kd[:, ks])
        ms = jnp.mean(o * o, axis=-1, keepdims=True)
        on = o * lax.rsqrt(ms + EPS) * nw
        o_ref[:, vs] = (on * _silu(g_ref[:, vs].astype(F32))).astype(BF16)


def _gla(proj, small, gla_w2, gla_b, gla_norm, bsz, s):
    C = GLA_CHUNK
    nc = s // C
    t = bsz * s
    hk = GLA_HEADS * GLA_KEY_DIM
    w2 = jnp.zeros((LANES, hk), F32).at[SM_LR:SM_LR + GLA_GATE_RANK].set(gla_w2.astype(F32))
    b2 = gla_b.astype(F32).reshape(1, hk)
    cmap = lambda name, w: (lambda b, c: (b * nc + c, _OFF[name] // w))
    const = lambda b, c: (0, 0)
    return pl.pallas_call(
        _gla_kernel,
        grid=(bsz, nc),
        in_specs=[
            pl.BlockSpec((C, hk), cmap("g_q", hk)),
            pl.BlockSpec((C, hk), cmap("g_k", hk)),
            pl.BlockSpec((C, BRANCH_WIDTH), cmap("g_v", BRANCH_WIDTH)),
            pl.BlockSpec((C, BRANCH_WIDTH), cmap("g_gate", BRANCH_WIDTH)),
            pl.BlockSpec((C, LANES), lambda b, c: (b * nc + c, 0)),
            pl.BlockSpec((LANES, hk), const),
            pl.BlockSpec((1, hk), const),
            pl.BlockSpec((1, GLA_VAL_DIM), const),
        ],
        out_specs=pl.BlockSpec((C, BRANCH_WIDTH), lambda b, c: (b * nc + c, 0)),
        out_shape=jax.ShapeDtypeStruct((t, BRANCH_WIDTH), BF16),
        scratch_shapes=[pltpu.VMEM((GLA_HEADS, GLA_VAL_DIM, GLA_KEY_DIM), F32)],
        compiler_params=_cparams(("parallel", "arbitrary")),
        name="gla",
    )(proj, proj, proj, proj, small, w2, b2, gla_norm.astype(F32).reshape(1, GLA_VAL_DIM))


def _mla_prep_kernel(cq_ref, ckv_ref, kr_ref, krs_ref, pos_ref, invf_ref, sign_ref, qn_ref,
                     kvn_ref, wqn_ref, wqr_ref, wqs_ref, wk_ref, wv_ref, q_ref, k_ref, v_ref):
    def norm(ref, w_ref):
        c = ref[...].astype(F32)
        ms = jnp.mean(c * c, axis=-1, keepdims=True)
        return (c * lax.rsqrt(ms + EPS) * w_ref[...]).astype(BF16)

    cq = norm(cq_ref, qn_ref)
    ckv = norm(ckv_ref, kvn_ref)
    ang = pos_ref[...] * invf_ref[...]
    cos = jnp.cos(ang)
    sin = jnp.sin(ang) * sign_ref[...]
    scale = (MLA_NOPE + MLA_ROPE) ** -0.5 * LOG2E

    qn = _dot(cq, wqn_ref[...]) * scale
    qr = _dot(cq, wqr_ref[...])
    qsw = _dot(cq, wqs_ref[...])
    kn = _dot(ckv, wk_ref[...])
    v_ref[...] = _dot(ckv, wv_ref[...]).astype(BF16)
    kr = (kr_ref[...] * cos + krs_ref[...] * sin).astype(BF16)
    for h in range(MLA_HEADS):
        lo, hi = h * LANES, (h + 1) * LANES
        base = h * MLA_QK_PAD
        q_ref[:, base:base + LANES] = qn[:, lo:hi].astype(BF16)
        q_ref[:, base + LANES:base + 2 * LANES] = (
            (qr[:, lo:hi] * cos + qsw[:, lo:hi] * sin) * scale).astype(BF16)
        k_ref[:, base:base + LANES] = kn[:, lo:hi].astype(BF16)
        k_ref[:, base + LANES:base + 2 * LANES] = kr


def _mla_prep(proj, small, posf, q_norm, kv_norm, w_uq, w_ukv, tm):
    t = proj.shape[0]
    half = MLA_ROPE // 2
    wq = w_uq.reshape(MLA_Q_RANK, MLA_HEADS, MLA_NOPE + MLA_ROPE)
    zpad = jnp.zeros((MLA_Q_RANK, MLA_HEADS, LANES - MLA_ROPE), w_uq.dtype)
    flat = lambda a: a.reshape(MLA_Q_RANK, MLA_HEADS * LANES).astype(BF16)
    wqn = flat(wq[:, :, :MLA_NOPE])
    wqr = flat(jnp.concatenate([wq[:, :, MLA_NOPE:], zpad], axis=2))
    wqs = flat(jnp.concatenate([wq[:, :, MLA_NOPE + half:], wq[:, :, MLA_NOPE:MLA_NOPE + half],
                                zpad], axis=2))
    wkv = w_ukv.reshape(MLA_KV_RANK, MLA_HEADS, MLA_NOPE + MLA_V)
    wk = flat(wkv[:, :, :MLA_NOPE])
    wv = flat(wkv[:, :, MLA_NOPE:])
    inv_freq = 1.0 / (ROPE_BASE ** (jnp.arange(half, dtype=F32) * 2.0 / MLA_ROPE))
    zero = jnp.zeros((LANES - MLA_ROPE,), F32)
    invf = jnp.concatenate([inv_freq, inv_freq, zero]).reshape(1, LANES)
    sign = jnp.concatenate([-jnp.ones((half,), F32), jnp.ones((half,), F32), zero]).reshape(1, LANES)
    wide = MLA_HEADS * MLA_QK_PAD
    row = lambda c: (lambda i: (i, c))
    const = lambda i: (0, 0)
    wspec = pl.BlockSpec((MLA_Q_RANK, MLA_HEADS * LANES), const)
    return pl.pallas_call(
        _mla_prep_kernel,
        grid=(t // tm,),
        in_specs=[
            pl.BlockSpec((tm, MLA_Q_RANK), row(_OFF["l_cq"] // MLA_Q_RANK)),
            pl.BlockSpec((tm, MLA_KV_RANK), row(_OFF["l_ckv"] // MLA_KV_RANK)),
            pl.BlockSpec((tm, LANES), row(1)),
            pl.BlockSpec((tm, LANES), row(2)),
            pl.BlockSpec((tm, 1), row(0)),
            pl.BlockSpec((1, LANES), const),
            pl.BlockSpec((1, LANES), const),
            pl.BlockSpec((1, MLA_Q_RANK), const),
            pl.BlockSpec((1, MLA_KV_RANK), const),
            wspec, wspec, wspec, wspec, wspec,
        ],
        out_specs=[
            pl.BlockSpec((tm, wide), row(0)),
            pl.BlockSpec((tm, wide), row(0)),
            pl.BlockSpec((tm, BRANCH_WIDTH), row(0)),
        ],
        out_shape=[
            jax.ShapeDtypeStruct((t, wide), BF16),
            jax.ShapeDtypeStruct((t, wide), BF16),
            jax.ShapeDtypeStruct((t, BRANCH_WIDTH), BF16),
        ],
        compiler_params=_cparams(("parallel",)),
        name="mla_prep",
    )(proj, proj, small, small, posf, invf, sign, q_norm.astype(F32).reshape(1, MLA_Q_RANK),
      kv_norm.astype(F32).reshape(1, MLA_KV_RANK), wqn, wqr, wqs, wk, wv)


def _merge_kernel(a_ref, b_ref, c_ref, d_ref, g0_ref, g1_ref, g2_ref, g3_ref, w_ref, o_ref):
    acc = None
    for br, (x_ref, g_ref) in enumerate(
            ((a_ref, g0_ref), (b_ref, g1_ref), (c_ref, g2_ref), (d_ref, g3_ref))):
        gate = 1.0 / (1.0 + jnp.exp(-g_ref[...].astype(F32)))
        term = gate * _dot(x_ref[...], w_ref[br])
        acc = term if acc is None else acc + term
    o_ref[...] = acc.astype(BF16)


def _merge(outs, proj, w_branch, tm, tn):
    t = proj.shape[0]
    nn = D_MODEL // tn
    xspec = pl.BlockSpec((tm, BRANCH_WIDTH), lambda i, n: (i, 0))
    gspec = lambda br: pl.BlockSpec((tm, tn), lambda i, n: (i, _OFF["merge"] // tn + br * nn + n))
    return pl.pallas_call(
        _merge_kernel,
        grid=(t // tm, nn),
        in_specs=[xspec] * 4 + [gspec(br) for br in range(N_BRANCHES)] + [
            pl.BlockSpec((N_BRANCHES, BRANCH_WIDTH, tn), lambda i, n: (0, 0, n))],
        out_specs=pl.BlockSpec((tm, tn), lambda i, n: (i, n)),
        out_shape=jax.ShapeDtypeStruct((t, D_MODEL), BF16),
        compiler_params=_cparams(("parallel", "arbitrary")),
        name="merge",
    )(*outs, proj, proj, proj, proj, w_branch.astype(BF16))


def _out_proj_kernel(m_ref, x_ref, w_ref, nw_ref, o_ref):
    y = _dot(m_ref[...], w_ref[...])
    ms = jnp.mean(y * y, axis=-1, keepdims=True)
    o_ref[...] = x_ref[...] + y * lax.rsqrt(ms + EPS) * nw_ref[...]


def _out_proj(mixed, x2d, w_out, post_norm, tm):
    t = x2d.shape[0]
    return pl.pallas_call(
        _out_proj_kernel,
        grid=(t // tm,),
        in_specs=[
            pl.BlockSpec((tm, D_MODEL), lambda i: (i, 0)),
            pl.BlockSpec((tm, D_MODEL), lambda i: (i, 0)),
            pl.BlockSpec((D_MODEL, D_MODEL), lambda i: (0, 0)),
            pl.BlockSpec((1, D_MODEL), lambda i: (0, 0)),
        ],
        out_specs=pl.BlockSpec((tm, D_MODEL), lambda i: (i, 0)),
        out_shape=jax.ShapeDtypeStruct((t, D_MODEL), F32),
        compiler_params=_cparams(("parallel",)),
        name="out_proj",
    )(mixed, x2d, w_out.astype(BF16), post_norm.astype(F32).reshape(1, D_MODEL))


def _pack_w_in(w_in):
    bounds = np.cumsum((0,) + _IN_SIZES)
    seg = {n: w_in[:, bounds[i]:bounds[i + 1]] for i, n in enumerate(_IN_NAMES)}
    big = jnp.concatenate([seg[n] for n in _BIG_ORDER], axis=1).astype(BF16)
    half = MLA_ROPE // 2
    kr = seg["l_kr"]
    z = lambda n: jnp.zeros((D_MODEL, n), w_in.dtype)
    small = jnp.concatenate([
        seg["m_dt"], seg["f_f"], seg["g_lr"], z(LANES - 48),
        kr, z(LANES - MLA_ROPE),
        kr[:, half:], kr[:, :half], z(LANES - MLA_ROPE)], axis=1).astype(BF16)
    return big, small


def _tile(n, want):
    return want if n % want == 0 else n


def _layer(x2d, posf, bsz, s, pre_norm, post_norm, w_in, conv_w, conv_b, dt_bias, a_log, d_skip,
           ssm_norm, fgate_b, gla_w2, gla_b, gla_norm, q_norm, kv_norm, w_uq, w_ukv, w_branch,
           w_out):
    t = bsz * s
    w_big, w_small = _pack_w_in(w_in)
    proj, small = _in_proj(x2d, pre_norm.astype(F32), w_big, w_small, _tile(t, 1024), 768)

    out_a = _ssd(proj, small, conv_w, conv_b, dt_bias, a_log, d_skip, ssm_norm, bsz, s)

    fcum = _fcum(small, fgate_b, bsz, s, _tile(s, 512))
    tq, tk = _tile(s, 1024), _tile(s, 512)
    pair = 2 * FOX_HEAD_DIM
    out_b = _flash(proj, _OFF["f_q"] // pair, proj, _OFF["f_k"] // pair, proj, _OFF["f_v"] // pair,
                   proj, _OFF["f_gate"] // pair, bsz, s, tq, tk, nsub=2, dk=FOX_HEAD_DIM,
                   dv=FOX_HEAD_DIM, q_scale=FOX_HEAD_DIM ** -0.5, fcum=fcum)

    out_c = _gla(proj, small, gla_w2, gla_b, gla_norm, bsz, s)

    q_full, k_full, v_mla = _mla_prep(proj, small, posf, q_norm, kv_norm, w_uq, w_ukv,
                                      _tile(t, 512))
    out_d = _flash(q_full, 0, k_full, 0, v_mla, 0, proj, _OFF["l_gate"] // MLA_V, bsz, s, tq, tk,
                   nsub=1, dk=MLA_QK_PAD, dv=MLA_V, q_scale=1.0)

    mixed = _merge((out_a, out_b, out_c, out_d), proj, w_branch, _tile(t, 1024), 512)
    return _out_proj(mixed, x2d, w_out, post_norm, _tile(t, 512))


def kernel(x, positions, pre_norm, post_norm, w_in, conv_w, conv_b, dt_bias, a_log, d_skip,
           ssm_norm, fgate_b, gla_w2, gla_b, gla_norm, q_norm, kv_norm, w_uq, w_ukv, w_branch,
           w_out):
    bsz, s, _ = x.shape
    x2d = x.reshape(bsz * s, D_MODEL)
    posf = positions.astype(F32).reshape(bsz * s, 1)
    for l in range(pre_norm.shape[0]):
        x2d = _layer(x2d, posf, bsz, s, pre_norm[l], post_norm[l], w_in[l], conv_w[l], conv_b[l],
                     dt_bias[l], a_log[l], d_skip[l], ssm_norm[l], fgate_b[l], gla_w2[l], gla_b[l],
                     gla_norm[l], q_norm[l], kv_norm[l], w_uq[l], w_ukv[l], w_branch[l], w_out[l])
    return x2d.reshape(bsz, s, D_MODEL)
```

```python
import functools

import jax
import jax.numpy as jnp
import numpy as np
from jax import lax
from jax.experimental import pallas as pl
from jax.experimental.pallas import tpu as pltpu

F32 = jnp.float32
BF16 = jnp.bfloat16
HIGHEST = lax.Precision.HIGHEST

D_MODEL = 2048
BRANCH_WIDTH = 1024
N_BRANCHES = 4
EPS = 1e-6

SSD_HEAD_DIM = 64
SSD_HEADS = 16
SSD_GROUPS = 2
SSD_STATE = 128
SSD_CONV = 4
SSD_CHUNK = 128
SSD_XBC = 1536

FOX_HEAD_DIM = 64
FOX_HEADS = 16

GLA_HEADS = 4
GLA_KEY_DIM = 128
GLA_VAL_DIM = 256
GLA_GATE_RANK = 16
GLA_GATE_TAU = 16.0
GLA_CHUNK = 64
GLA_BAND = 16

MLA_HEADS = 8
MLA_NOPE = 128
MLA_ROPE = 64
MLA_V = 128
MLA_Q_RANK = 512
MLA_KV_RANK = 512
MLA_QK_PAD = 256
ROPE_BASE = 10000.0

LANES = 128
VMEM_LIMIT = 56 * 1024 * 1024

_IN_NAMES = ("m_z", "m_xbc", "m_dt", "f_q", "f_k", "f_v", "f_f", "f_gate",
             "g_q", "g_k", "g_v", "g_lr", "g_gate", "l_cq", "l_ckv", "l_kr", "l_gate", "merge")
_IN_SIZES = (1024, 1536, 16, 1024, 1024, 1024, 16, 1024,
             512, 512, 1024, 16, 1024, 512, 512, 64, 1024, 8192)
_BIG_ORDER = ("merge", "m_z", "f_q", "f_k", "f_v", "f_gate", "g_v", "g_gate", "l_gate",
              "g_q", "g_k", "l_cq", "l_ckv", "m_xbc")
_SIZE = dict(zip(_IN_NAMES, _IN_SIZES))
_OFF = {}
_o = 0
for _n in _BIG_ORDER:
    _OFF[_n] = _o
    _o += _SIZE[_n]
N_BIG = _o
N_SMALL = 3 * LANES
SM_DT, SM_FF, SM_LR = 0, 16, 32


def _cparams(sem):
    return pltpu.CompilerParams(dimension_semantics=sem, vmem_limit_bytes=VMEM_LIMIT)


def _silu(v):
    return v / (1.0 + jnp.exp(-v))


def _softplus(v):
    return jnp.maximum(v, 0.0) + jnp.log1p(jnp.exp(-jnp.abs(v)))


def _log_sigmoid(v):
    return jnp.minimum(v, 0.0) - jnp.log1p(jnp.exp(-jnp.abs(v)))


def _tril(n):
    r = lax.broadcasted_iota(jnp.int32, (n, n), 0)
    c = lax.broadcasted_iota(jnp.int32, (n, n), 1)
    return r >= c


def _dot(a, b):
    return jnp.dot(a, b, preferred_element_type=F32)


def _dot_nt(a, b):
    return lax.dot_general(a, b, (((1,), (1,)), ((), ())), preferred_element_type=F32)


def _dot_tn(a, b):
    return lax.dot_general(a, b, (((0,), (0,)), ((), ())), preferred_element_type=F32)


def _dot_hi(a, b):
    return jnp.dot(a, b, precision=HIGHEST, preferred_element_type=F32)


def _split3(v):
    hi = v.astype(BF16).astype(F32)
    mid = (v - hi).astype(BF16).astype(F32)
    lo = v - hi - mid
    return hi, mid, lo


def _dot_sel_lhs(sel, x):
    s = sel.astype(BF16)
    hi, mid, lo = _split3(x)
    return _dot(s, hi.astype(BF16)) + _dot(s, mid.astype(BF16)) + _dot(s, lo.astype(BF16))


def _dot_sel_rhs(x, sel):
    hi, mid, lo = _split3(x)
    return _dot(hi.astype(BF16), sel) + _dot(mid.astype(BF16), sel) + _dot(lo.astype(BF16), sel)


def _in_proj_kernel(x_ref, pn_ref, w_ref, ws_ref, o_ref, os_ref, h_ref):
    @pl.when(pl.program_id(1) == 0)
    def _():
        x = x_ref[...]
        ms = jnp.mean(x * x, axis=-1, keepdims=True)
        hb = (x * lax.rsqrt(ms + EPS) * pn_ref[...]).astype(BF16)
        h_ref[...] = hb
        os_ref[...] = _dot(hb, ws_ref[...])

    o_ref[...] = _dot(h_ref[...], w_ref[...]).astype(BF16)


def _in_proj(x2d, pre_norm, w_big, w_small, tm, tn):
    t = x2d.shape[0]
    return pl.pallas_call(
        _in_proj_kernel,
        grid=(t // tm, N_BIG // tn),
        in_specs=[
            pl.BlockSpec((tm, D_MODEL), lambda i, j: (i, 0)),
            pl.BlockSpec((1, D_MODEL), lambda i, j: (0, 0)),
            pl.BlockSpec((D_MODEL, tn), lambda i, j: (0, j)),
            pl.BlockSpec((D_MODEL, N_SMALL), lambda i, j: (0, 0)),
        ],
        out_specs=[
            pl.BlockSpec((tm, tn), lambda i, j: (i, j)),
            pl.BlockSpec((tm, N_SMALL), lambda i, j: (i, 0)),
        ],
        out_shape=[
            jax.ShapeDtypeStruct((t, N_BIG), BF16),
            jax.ShapeDtypeStruct((t, N_SMALL), F32),
        ],
        scratch_shapes=[pltpu.VMEM((tm, D_MODEL), BF16)],
        compiler_params=_cparams(("parallel", "arbitrary")),
        name="in_proj",
    )(x2d, pre_norm.reshape(1, D_MODEL), w_big, w_small)


def _fcum_kernel(s_ref, fb_ref, o_ref, carry_ref, *, tc):
    @pl.when(pl.program_id(1) == 0)
    def _():
        carry_ref[...] = jnp.zeros_like(carry_ref)

    lf = _log_sigmoid(s_ref[...] + fb_ref[...])
    cum = _dot_sel_lhs(_tril(tc), lf) + carry_ref[...]
    o_ref[...] = cum
    carry_ref[...] = cum[tc - 1:tc, :]


def _fcum(small, fgate_b, bsz, s, tc):
    fb = jnp.zeros((1, LANES), F32).at[0, SM_FF:SM_FF + FOX_HEADS].set(fgate_b.astype(F32))
    nj = s // tc
    return pl.pallas_call(
        functools.partial(_fcum_kernel, tc=tc),
        grid=(bsz, nj),
        in_specs=[
            pl.BlockSpec((tc, LANES), lambda b, j: (b * nj + j, 0)),
            pl.BlockSpec((1, LANES), lambda b, j: (0, 0)),
        ],
        out_specs=pl.BlockSpec((tc, LANES), lambda b, j: (b * nj + j, 0)),
        out_shape=jax.ShapeDtypeStruct((bsz * s, LANES), F32),
        scratch_shapes=[pltpu.VMEM((1, LANES), F32)],
        compiler_params=_cparams(("parallel", "arbitrary")),
        name="fcum",
    )(small, fb)


def _ssd_kernel(z_ref, xbc_ref, sm_ref, cw_ref, cb_ref, dtb_ref, alog_ref, e_ref, dsk_ref,
                nw_ref, o_ref, tail_ref, state_ref, y_ref):
    L = SSD_CHUNK
    gw = BRANCH_WIDTH // SSD_GROUPS
    hpg = SSD_HEADS // SSD_GROUPS

    @pl.when(pl.program_id(1) == 0)
    def _():
        tail_ref[...] = jnp.zeros_like(tail_ref)
        state_ref[...] = jnp.zeros_like(state_ref)

    cur = xbc_ref[...].astype(F32)
    xcat = jnp.concatenate([tail_ref[...], cur], axis=0)
    w = cw_ref[...]
    y = cb_ref[...]
    for k in range(SSD_CONV):
        y = y + w[k:k + 1, :] * xcat[5 + k:5 + k + L, :]
    tail_ref[...] = cur[L - 8:L, :]
    xbc = _silu(y)
    xs = xbc[:, :BRANCH_WIDTH]
    bm = xbc[:, BRANCH_WIDTH:BRANCH_WIDTH + SSD_GROUPS * SSD_STATE]
    cm = xbc[:, BRANCH_WIDTH + SSD_GROUPS * SSD_STATE:]

    lane = lax.broadcasted_iota(jnp.int32, (L, LANES), 1)
    head_lane = lane < SSD_HEADS
    dt = jnp.where(head_lane, _softplus(sm_ref[...] + dtb_ref[...]), 0.0)
    la = dt * (-jnp.exp(alog_ref[...]))
    causal = _tril(L)
    acum = _dot_sel_lhs(causal, la)
    acum_t = acum.T
    a_last = acum[L - 1:L, :]
    e = e_ref[...]
    dt_e = _dot_sel_rhs(dt, e)
    eac_e = jnp.exp(_dot_sel_rhs(acum, e))
    dte_e = jnp.exp(_dot_sel_rhs(a_last - acum, e))
    xdt = xs * dt_e
    xw = (xdt * dte_e).astype(BF16)
    xdt_b = xdt.astype(BF16)

    yoff = []
    for g in range(SSD_GROUPS):
        bg = bm[:, g * SSD_STATE:(g + 1) * SSD_STATE].astype(BF16)
        cg = cm[:, g * SSD_STATE:(g + 1) * SSD_STATE].astype(BF16)
        cb = _dot_nt(cg, bg)
        st = state_ref[:, g * gw:(g + 1) * gw]
        yoff.append(_dot(cg, st.astype(BF16)) * eac_e[:, g * gw:(g + 1) * gw])
        state_ref[:, g * gw:(g + 1) * gw] = (
            st * eac_e[L - 1:L, g * gw:(g + 1) * gw] + _dot_tn(bg, xw[:, g * gw:(g + 1) * gw]))
        for r in range(hpg):
            h = g * hpg + r
            seg = acum[:, h:h + 1] - acum_t[h:h + 1, :]
            dec = jnp.where(causal, jnp.exp(jnp.minimum(seg, 0.0)), 0.0)
            m = (cb * dec).astype(BF16)
            y_ref[:, h * SSD_HEAD_DIM:(h + 1) * SSD_HEAD_DIM] = _dot(
                m, xdt_b[:, h * SSD_HEAD_DIM:(h + 1) * SSD_HEAD_DIM])

    yt = y_ref[...] + jnp.concatenate(yoff, axis=1) + xs * dsk_ref[...]
    yt = yt * _silu(z_ref[...].astype(F32))
    nw = nw_ref[...]
    for g in range(SSD_GROUPS):
        yg = yt[:, g * gw:(g + 1) * gw]
        ms = jnp.mean(yg * yg, axis=-1, keepdims=True)
        o_ref[:, g * gw:(g + 1) * gw] = (
            yg * lax.rsqrt(ms + EPS) * nw[:, g * gw:(g + 1) * gw]).astype(BF16)


def _ssd(proj, small, conv_w, conv_b, dt_bias, a_log, d_skip, ssm_norm, bsz, s):
    L = SSD_CHUNK
    nc = s // L
    t = bsz * s
    pad = lambda v: jnp.zeros((1, LANES), F32).at[0, :SSD_HEADS].set(v.astype(F32))
    expand = jnp.zeros((LANES, BRANCH_WIDTH), BF16).at[:SSD_HEADS].set(
        jnp.repeat(jnp.eye(SSD_HEADS, dtype=BF16), SSD_HEAD_DIM, axis=1))
    dsk = jnp.repeat(d_skip.astype(F32), SSD_HEAD_DIM).reshape(1, BRANCH_WIDTH)
    row = lambda b, c: (b * nc + c, 0)
    const = lambda b, c: (0, 0)
    return pl.pallas_call(
        _ssd_kernel,
        grid=(bsz, nc),
        in_specs=[
            pl.BlockSpec((L, BRANCH_WIDTH), lambda b, c: (b * nc + c, _OFF["m_z"] // BRANCH_WIDTH)),
            pl.BlockSpec((L, SSD_XBC), lambda b, c: (b * nc + c, _OFF["m_xbc"] // SSD_XBC)),
            pl.BlockSpec((L, LANES), row),
            pl.BlockSpec((SSD_CONV, SSD_XBC), const),
            pl.BlockSpec((1, SSD_XBC), const),
            pl.BlockSpec((1, LANES), const),
            pl.BlockSpec((1, LANES), const),
            pl.BlockSpec((LANES, BRANCH_WIDTH), const),
            pl.BlockSpec((1, BRANCH_WIDTH), const),
            pl.BlockSpec((1, BRANCH_WIDTH), const),
        ],
        out_specs=pl.BlockSpec((L, BRANCH_WIDTH), row),
        out_shape=jax.ShapeDtypeStruct((t, BRANCH_WIDTH), BF16),
        scratch_shapes=[
            pltpu.VMEM((8, SSD_XBC), F32),
            pltpu.VMEM((SSD_STATE, BRANCH_WIDTH), F32),
            pltpu.VMEM((L, BRANCH_WIDTH), F32),
        ],
        compiler_params=_cparams(("parallel", "arbitrary")),
        name="ssd",
    )(proj, proj, small, conv_w.astype(F32), conv_b.astype(F32).reshape(1, SSD_XBC),
      pad(dt_bias), pad(a_log), expand, dsk, ssm_norm.astype(F32).reshape(1, BRANCH_WIDTH))


NEG_BIG = -1e30
LOG2E = 1.4426950408889634


def _flash_kernel(*refs, nsub, dk, dv, tq, tk, seq, has_bias, q_scale, head_lane0):
    if has_bias:
        q_ref, k_ref, v_ref, fq_ref, fs_ref, g_ref, o_ref, vt_ref, ka_ref = refs
    else:
        q_ref, k_ref, v_ref, g_ref, o_ref, vt_ref = refs
    hp = pl.program_id(1)
    qi = pl.program_id(2)
    ratio = tq // tk

    def with_bias_lanes(x, f_col, s, query_side):
        rows = x.shape[0]
        lane = lax.broadcasted_iota(jnp.int32, (rows, LANES), 1)
        own = (lane >= s * dk) & (lane < (s + 1) * dk)
        c0 = (1 - s) * dk
        hi, mid, lo = _split3(f_col * LOG2E)
        one = jnp.ones_like(f_col)
        pieces = (hi, mid, lo, one, one, one) if query_side else (one, one, one, -hi, -mid, -lo)
        extra = jnp.zeros((rows, LANES), F32)
        for j, pc in enumerate(pieces):
            extra = jnp.where(lane == c0 + j, pc, extra)
        return jnp.where(own, x, extra).astype(BF16)

    def pick_head(f, s):
        lane = lax.broadcasted_iota(jnp.int32, f.shape, 1)
        return jnp.sum(jnp.where(lane == head_lane0 + hp * nsub + s, f, 0.0), axis=1,
                       keepdims=True)

    @pl.when(qi == 0)
    def _():
        for c in range(seq // tk):
            rows = slice(c * tk, (c + 1) * tk)
            vt_ref[c] = v_ref[rows, :].astype(F32).T.astype(BF16)
            if has_bias:
                kf = k_ref[rows, :].astype(F32)
                f = fs_ref[rows, :]
                for s in range(nsub):
                    ka_ref[s, rows, :] = with_bias_lanes(kf, pick_head(f, s), s, False)

    if has_bias:
        qf = q_ref[...].astype(F32) * (q_scale * LOG2E)
        fq = fq_ref[...]
        qa = [with_bias_lanes(qf, pick_head(fq, s), s, True) for s in range(nsub)]
    else:
        qa = [q_ref[:, s * dk:(s + 1) * dk] for s in range(nsub)]

    def block(kj, carry, lane0, masked):
        w = tq - lane0
        out = []
        for s in range(nsub):
            m, l, acc = carry[s]
            rows = pl.ds(pl.multiple_of(kj * tk, tk), tk)
            kb = ka_ref[s, rows, :] if has_bias else k_ref[rows, s * dk:(s + 1) * dk]
            st = _dot_nt(kb, qa[s][lane0:, :])
            if masked:
                key = lax.broadcasted_iota(jnp.int32, (tk, w), 0)
                qry = lax.broadcasted_iota(jnp.int32, (tk, w), 1)
                st = jnp.where(key <= qry, st, NEG_BIG)
            m_old = m[:, lane0:]
            m_new = jnp.maximum(m_old, jnp.max(st, axis=0, keepdims=True))
            p = jnp.exp2(st - m_new)
            alpha = jnp.exp2(m_old - m_new)
            l_new = alpha * l[:, lane0:] + jnp.sum(p, axis=0, keepdims=True)
            vt = vt_ref[kj, s * dv:(s + 1) * dv, :]
            acc_new = alpha * acc[:, lane0:] + _dot(vt, p.astype(BF16))
            if lane0:
                m_new = jnp.concatenate([m[:, :lane0], m_new], axis=1)
                l_new = jnp.concatenate([l[:, :lane0], l_new], axis=1)
                acc_new = jnp.concatenate([acc[:, :lane0], acc_new], axis=1)
            out.append((m_new, l_new, acc_new))
        return tuple(out)

    init = tuple((jnp.full((1, tq), NEG_BIG, F32), jnp.zeros((1, tq), F32),
                  jnp.zeros((dv, tq), F32)) for _ in range(nsub))
    carry = lax.fori_loop(0, qi * ratio, lambda j, c: block(j, c, 0, False), init)
    for d in range(ratio):
        carry = block(qi * ratio + d, carry, d * tk, True)
    ot = jnp.concatenate([acc / l for (_, l, acc) in carry], axis=0)
    o_ref[...] = (ot.T * _silu(g_ref[...].astype(F32))).astype(BF16)


def _flash(q_arr, q_col0, k_arr, k_col0, v_arr, v_col0, gate_arr, gate_col0, bsz, s, tq, tk,
           nsub, dk, dv, q_scale, fcum=None):
    nq = s // tq
    nh = BRANCH_WIDTH // (nsub * dv)
    qmap = lambda c0: (lambda b, h, i: (b * nq + i, c0 + h))
    kvmap = lambda c0: (lambda b, h, i: (b, c0 + h))
    in_specs = [
        pl.BlockSpec((tq, nsub * dk), qmap(q_col0)),
        pl.BlockSpec((s, nsub * dk), kvmap(k_col0)),
        pl.BlockSpec((s, nsub * dv), kvmap(v_col0)),
    ]
    args = [q_arr, k_arr, v_arr]
    scratch = [pltpu.VMEM((s // tk, nsub * dv, tk), BF16)]
    if fcum is not None:
        in_specs += [
            pl.BlockSpec((tq, LANES), lambda b, h, i: (b * nq + i, 0)),
            pl.BlockSpec((s, LANES), lambda b, h, i: (b, 0)),
        ]
        args += [fcum, fcum]
        scratch.append(pltpu.VMEM((nsub, s, LANES), BF16))
    in_specs.append(pl.BlockSpec((tq, nsub * dv), qmap(gate_col0)))
    args.append(gate_arr)
    return pl.pallas_call(
        functools.partial(_flash_kernel, nsub=nsub, dk=dk, dv=dv, tq=tq, tk=tk, seq=s,
                          has_bias=fcum is not None, q_scale=q_scale, head_lane0=SM_FF),
        grid=(bsz, nh, nq),
        in_specs=in_specs,
        out_specs=pl.BlockSpec((tq, nsub * dv), lambda b, h, i: (b * nq + i, h)),
        out_shape=jax.ShapeDtypeStruct((bsz * s, BRANCH_WIDTH), BF16),
        scratch_shapes=scratch,
        compiler_params=_cparams(("parallel", "parallel", "arbitrary")),
        name="flash_fox" if fcum is not None else "flash_mla",
    )(*args)


def _gla_kernel(q_ref, k_ref, v_ref, g_ref, sm_ref, w2_ref, b2_ref, nw_ref, o_ref, st_ref):
    C, H, DK, DV = GLA_CHUNK, GLA_HEADS, GLA_KEY_DIM, GLA_VAL_DIM

    @pl.when(pl.program_id(1) == 0)
    def _():
        st_ref[...] = jnp.zeros_like(st_ref)

    la = _log_sigmoid(_dot_hi(sm_ref[...], w2_ref[...]) + b2_ref[...]) * (1.0 / GLA_GATE_TAU)
    bcum = _dot_sel_lhs(_tril(C), la)
    q = q_ref[...].astype(F32) * (DK ** -0.5)
    k = k_ref[...].astype(F32)
    b_last = bcum[C - 1:C, :]
    qe = (q * jnp.exp(bcum)).astype(BF16)
    kd = (k * jnp.exp(b_last - bcum)).astype(BF16)
    decay_all = jnp.exp(b_last)

    rowi = lax.broadcasted_iota(jnp.int32, (C, C), 0)
    coli = lax.broadcasted_iota(jnp.int32, (C, C), 1)
    dist = rowi - coli
    attn = [jnp.zeros((C, C), F32) for _ in range(H)]
    for d in range(GLA_BAND):
        kr = k if d == 0 else pltpu.roll(k, d, 0)
        br = bcum if d == 0 else pltpu.roll(bcum, d, 0)
        term = q * kr * jnp.exp(jnp.minimum(bcum - br, 0.0))
        for h in range(H):
            dd = jnp.sum(term[:, h * DK:(h + 1) * DK], axis=-1, keepdims=True)
            attn[h] = attn[h] + jnp.where(dist == d, dd, 0.0)
    far = [[jnp.zeros((GLA_BAND, C), F32)] for _ in range(H)]
    for i0 in range(GLA_BAND, C, GLA_BAND):
        bref = bcum[i0 - 1:i0, :]
        qi = (q[i0:i0 + GLA_BAND] * jnp.exp(bcum[i0:i0 + GLA_BAND] - bref)).astype(BF16)
        kj = (k * jnp.exp(jnp.minimum(bref - bcum, 0.0))).astype(BF16)
        for h in range(H):
            far[h].append(_dot_nt(qi[:, h * DK:(h + 1) * DK], kj[:, h * DK:(h + 1) * DK]))

    nw = nw_ref[...]
    for h in range(H):
        ks, vs = slice(h * DK, (h + 1) * DK), slice(h * DV, (h + 1) * DV)
        a = attn[h] + jnp.where(dist >= GLA_BAND, jnp.concatenate(far[h], axis=0), 0.0)
        v = v_ref[:, vs]
        st = st_ref[h]
        o = _dot_nt(qe[:, ks], st.astype(BF16)) + _dot(a.astype(BF16), v)
        st_ref[h] = st * decay_all[:, ks] + _dot_tn(v, kd[:, ks])
        ms = jnp.mean(o * o, axis=-1, keepdims=True)
        on = o * lax.rsqrt(ms + EPS) * nw
        o_ref[:, vs] = (on * _silu(g_ref[:, vs].astype(F32))).astype(BF16)


def _gla(proj, small, gla_w2, gla_b, gla_norm, bsz, s):
    C = GLA_CHUNK
    nc = s // C
    t = bsz * s
    hk = GLA_HEADS * GLA_KEY_DIM
    w2 = jnp.zeros((LANES, hk), F32).at[SM_LR:SM_LR + GLA_GATE_RANK].set(gla_w2.astype(F32))
    b2 = gla_b.astype(F32).reshape(1, hk)
    cmap = lambda name, w: (lambda b, c: (b * nc + c, _OFF[name] // w))
    const = lambda b, c: (0, 0)
    return pl.pallas_call(
        _gla_kernel,
        grid=(bsz, nc),
        in_specs=[
            pl.BlockSpec((C, hk), cmap("g_q", hk)),
            pl.BlockSpec((C, hk), cmap("g_k", hk)),
            pl.BlockSpec((C, BRANCH_WIDTH), cmap("g_v", BRANCH_WIDTH)),
            pl.BlockSpec((C, BRANCH_WIDTH), cmap("g_gate", BRANCH_WIDTH)),
            pl.BlockSpec((C, LANES), lambda b, c: (b * nc + c, 0)),
            pl.BlockSpec((LANES, hk), const),
            pl.BlockSpec((1, hk), const),
            pl.BlockSpec((1, GLA_VAL_DIM), const),
        ],
        out_specs=pl.BlockSpec((C, BRANCH_WIDTH), lambda b, c: (b * nc + c, 0)),
        out_shape=jax.ShapeDtypeStruct((t, BRANCH_WIDTH), BF16),
        scratch_shapes=[pltpu.VMEM((GLA_HEADS, GLA_VAL_DIM, GLA_KEY_DIM), F32)],
        compiler_params=_cparams(("parallel", "arbitrary")),
        name="gla",
    )(proj, proj, proj, proj, small, w2, b2, gla_norm.astype(F32).reshape(1, GLA_VAL_DIM))


def _mla_prep_kernel(cq_ref, ckv_ref, kr_ref, krs_ref, pos_ref, invf_ref, sign_ref, qn_ref,
                     kvn_ref, wqn_ref, wqr_ref, wqs_ref, wk_ref, wv_ref, q_ref, k_ref, v_ref):
    def norm(ref, w_ref):
        c = ref[...].astype(F32)
        ms = jnp.mean(c * c, axis=-1, keepdims=True)
        return (c * lax.rsqrt(ms + EPS) * w_ref[...]).astype(BF16)

    cq = norm(cq_ref, qn_ref)
    ckv = norm(ckv_ref, kvn_ref)
    ang = pos_ref[...] * invf_ref[...]
    cos = jnp.cos(ang)
    sin = jnp.sin(ang) * sign_ref[...]
    scale = (MLA_NOPE + MLA_ROPE) ** -0.5 * LOG2E

    qn = _dot(cq, wqn_ref[...]) * scale
    qr = _dot(cq, wqr_ref[...])
    qsw = _dot(cq, wqs_ref[...])
    kn = _dot(ckv, wk_ref[...])
    v_ref[...] = _dot(ckv, wv_ref[...]).astype(BF16)
    kr = (kr_ref[...] * cos + krs_ref[...] * sin).astype(BF16)
    for h in range(MLA_HEADS):
        lo, hi = h * LANES, (h + 1) * LANES
        base = h * MLA_QK_PAD
        q_ref[:, base:base + LANES] = qn[:, lo:hi].astype(BF16)
        q_ref[:, base + LANES:base + 2 * LANES] = (
            (qr[:, lo:hi] * cos + qsw[:, lo:hi] * sin) * scale).astype(BF16)
        k_ref[:, base:base + LANES] = kn[:, lo:hi].astype(BF16)
        k_ref[:, base + LANES:base + 2 * LANES] = kr


def _mla_prep(proj, small, posf, q_norm, kv_norm, w_uq, w_ukv, tm):
    t = proj.shape[0]
    half = MLA_ROPE // 2
    wq = w_uq.reshape(MLA_Q_RANK, MLA_HEADS, MLA_NOPE + MLA_ROPE)
    zpad = jnp.zeros((MLA_Q_RANK, MLA_HEADS, LANES - MLA_ROPE), w_uq.dtype)
    flat = lambda a: a.reshape(MLA_Q_RANK, MLA_HEADS * LANES).astype(BF16)
    wqn = flat(wq[:, :, :MLA_NOPE])
    wqr = flat(jnp.concatenate([wq[:, :, MLA_NOPE:], zpad], axis=2))
    wqs = flat(jnp.concatenate([wq[:, :, MLA_NOPE + half:], wq[:, :, MLA_NOPE:MLA_NOPE + half],
                                zpad], axis=2))
    wkv = w_ukv.reshape(MLA_KV_RANK, MLA_HEADS, MLA_NOPE + MLA_V)
    wk = flat(wkv[:, :, :MLA_NOPE])
    wv = flat(wkv[:, :, MLA_NOPE:])
    inv_freq = 1.0 / (ROPE_BASE ** (jnp.arange(half, dtype=F32) * 2.0 / MLA_ROPE))
    zero = jnp.zeros((LANES - MLA_ROPE,), F32)
    invf = jnp.concatenate([inv_freq, inv_freq, zero]).reshape(1, LANES)
    sign = jnp.concatenate([-jnp.ones((half,), F32), jnp.ones((half,), F32), zero]).reshape(1, LANES)
    wide = MLA_HEADS * MLA_QK_PAD
    row = lambda c: (lambda i: (i, c))
    const = lambda i: (0, 0)
    wspec = pl.BlockSpec((MLA_Q_RANK, MLA_HEADS * LANES), const)
    return pl.pallas_call(
        _mla_prep_kernel,
        grid=(t // tm,),
        in_specs=[
            pl.BlockSpec((tm, MLA_Q_RANK), row(_OFF["l_cq"] // MLA_Q_RANK)),
            pl.BlockSpec((tm, MLA_KV_RANK), row(_OFF["l_ckv"] // MLA_KV_RANK)),
            pl.BlockSpec((tm, LANES), row(1)),
            pl.BlockSpec((tm, LANES), row(2)),
            pl.BlockSpec((tm, 1), row(0)),
            pl.BlockSpec((1, LANES), const),
            pl.BlockSpec((1, LANES), const),
            pl.BlockSpec((1, MLA_Q_RANK), const),
            pl.BlockSpec((1, MLA_KV_RANK), const),
            wspec, wspec, wspec, wspec, wspec,
        ],
        out_specs=[
            pl.BlockSpec((tm, wide), row(0)),
            pl.BlockSpec((tm, wide), row(0)),
            pl.BlockSpec((tm, BRANCH_WIDTH), row(0)),
        ],
        out_shape=[
            jax.ShapeDtypeStruct((t, wide), BF16),
            jax.ShapeDtypeStruct((t, wide), BF16),
            jax.ShapeDtypeStruct((t, BRANCH_WIDTH), BF16),
        ],
        compiler_params=_cparams(("parallel",)),
        name="mla_prep",
    )(proj, proj, small, small, posf, invf, sign, q_norm.astype(F32).reshape(1, MLA_Q_RANK),
      kv_norm.astype(F32).reshape(1, MLA_KV_RANK), wqn, wqr, wqs, wk, wv)


def _merge_kernel(a_ref, b_ref, c_ref, d_ref, g0_ref, g1_ref, g2_ref, g3_ref, w_ref, o_ref):
    acc = None
    for br, (x_ref, g_ref) in enumerate(
            ((a_ref, g0_ref), (b_ref, g1_ref), (c_ref, g2_ref), (d_ref, g3_ref))):
        gate = 1.0 / (1.0 + jnp.exp(-g_ref[...].astype(F32)))
        term = gate * _dot(x_ref[...], w_ref[br])
        acc = term if acc is None else acc + term
    o_ref[...] = acc.astype(BF16)


def _merge(outs, proj, w_branch, tm, tn):
    t = proj.shape[0]
    nn = D_MODEL // tn
    xspec = pl.BlockSpec((tm, BRANCH_WIDTH), lambda i, n: (i, 0))
    gspec = lambda br: pl.BlockSpec((tm, tn), lambda i, n: (i, _OFF["merge"] // tn + br * nn + n))
    return pl.pallas_call(
        _merge_kernel,
        grid=(t // tm, nn),
        in_specs=[xspec] * 4 + [gspec(br) for br in range(N_BRANCHES)] + [
            pl.BlockSpec((N_BRANCHES, BRANCH_WIDTH, tn), lambda i, n: (0, 0, n))],
        out_specs=pl.BlockSpec((tm, tn), lambda i, n: (i, n)),
        out_shape=jax.ShapeDtypeStruct((t, D_MODEL), BF16),
        compiler_params=_cparams(("parallel", "arbitrary")),
        name="merge",
    )(*outs, proj, proj, proj, proj, w_branch.astype(BF16))


def _out_proj_kernel(m_ref, x_ref, w_ref, nw_ref, o_ref):
    y = _dot(m_ref[...], w_ref[...])
    ms = jnp.mean(y * y, axis=-1, keepdims=True)
    o_ref[...] = x_ref[...] + y * lax.rsqrt(ms + EPS) * nw_ref[...]


def _out_proj(mixed, x2d, w_out, post_norm, tm):
    t = x2d.shape[0]
    return pl.pallas_call(
        _out_proj_kernel,
        grid=(t // tm,),
        in_specs=[
            pl.BlockSpec((tm, D_MODEL), lambda i: (i, 0)),
            pl.BlockSpec((tm, D_MODEL), lambda i: (i, 0)),
            pl.BlockSpec((D_MODEL, D_MODEL), lambda i: (0, 0)),
            pl.BlockSpec((1, D_MODEL), lambda i: (0, 0)),
        ],
        out_specs=pl.BlockSpec((tm, D_MODEL), lambda i: (i, 0)),
        out_shape=jax.ShapeDtypeStruct((t, D_MODEL), F32),
        compiler_params=_cparams(("parallel",)),
        name="out_proj",
    )(mixed, x2d, w_out.astype(BF16), post_norm.astype(F32).reshape(1, D_MODEL))


def _pack_w_in(w_in):
    bounds = np.cumsum((0,) + _IN_SIZES)
    seg = {n: w_in[:, bounds[i]:bounds[i + 1]] for i, n in enumerate(_IN_NAMES)}
    big = jnp.concatenate([seg[n] for n in _BIG_ORDER], axis=1).astype(BF16)
    half = MLA_ROPE // 2
    kr = seg["l_kr"]
    z = lambda n: jnp.zeros((D_MODEL, n), w_in.dtype)
    small = jnp.concatenate([
        seg["m_dt"], seg["f_f"], seg["g_lr"], z(LANES - 48),
        kr, z(LANES - MLA_ROPE),
        kr[:, half:], kr[:, :half], z(LANES - MLA_ROPE)], axis=1).astype(BF16)
    return big, small


def _tile(n, want):
    return want if n % want == 0 else n


def _layer(x2d, posf, bsz, s, pre_norm, post_norm, w_in, conv_w, conv_b, dt_bias, a_log, d_skip,
           ssm_norm, fgate_b, gla_w2, gla_b, gla_norm, q_norm, kv_norm, w_uq, w_ukv, w_branch,
           w_out):
    t = bsz * s
    w_big, w_small = _pack_w_in(w_in)
    proj, small = _in_proj(x2d, pre_norm.astype(F32), w_big, w_small, _tile(t, 1024), 1536)

    out_a = _ssd(proj, small, conv_w, conv_b, dt_bias, a_log, d_skip, ssm_norm, bsz, s)

    fcum = _fcum(small, fgate_b, bsz, s, _tile(s, 512))
    tq, tk = _tile(s, 2048), _tile(s, 512)
    pair = 2 * FOX_HEAD_DIM
    out_b = _flash(proj, _OFF["f_q"] // pair, proj, _OFF["f_k"] // pair, proj, _OFF["f_v"] // pair,
                   proj, _OFF["f_gate"] // pair, bsz, s, tq, tk, nsub=2, dk=FOX_HEAD_DIM,
                   dv=FOX_HEAD_DIM, q_scale=FOX_HEAD_DIM ** -0.5, fcum=fcum)

    out_c = _gla(proj, small, gla_w2, gla_b, gla_norm, bsz, s)

    q_full, k_full, v_mla = _mla_prep(proj, small, posf, q_norm, kv_norm, w_uq, w_ukv,
                                      _tile(t, 512))
    out_d = _flash(q_full, 0, k_full, 0, v_mla, 0, proj, _OFF["l_gate"] // (2 * MLA_V), bsz, s,
                   tq, tk, nsub=2, dk=MLA_QK_PAD, dv=MLA_V, q_scale=1.0)

    mixed = _merge((out_a, out_b, out_c, out_d), proj, w_branch, _tile(t, 1024), 512)
    return _out_proj(mixed, x2d, w_out, post_norm, _tile(t, 512))


def kernel(x, positions, pre_norm, post_norm, w_in, conv_w, conv_b, dt_bias, a_log, d_skip,
           ssm_norm, fgate_b, gla_w2, gla_b, gla_norm, q_norm, kv_norm, w_uq, w_ukv, w_branch,
           w_out):
    bsz, s, _ = x.shape
    x2d = x.reshape(bsz * s, D_MODEL)
    posf = positions.astype(F32).reshape(bsz * s, 1)
    for l in range(pre_norm.shape[0]):
        x2d = _layer(x2d, posf, bsz, s, pre_norm[l], post_norm[l], w_in[l], conv_w[l], conv_b[l],
                     dt_bias[l], a_log[l], d_skip[l], ssm_norm[l], fgate_b[l], gla_w2[l], gla_b[l],
                     gla_norm[l], q_norm[l], kv_norm[l], w_uq[l], w_ukv[l], w_branch[l], w_out[l])
    return x2d.reshape(bsz, s, D_MODEL)
```

```python
import functools

import jax
import jax.numpy as jnp
import numpy as np
from jax import lax
from jax.experimental import pallas as pl
from jax.experimental.pallas import tpu as pltpu

F32 = jnp.float32
BF16 = jnp.bfloat16
HIGHEST = lax.Precision.HIGHEST

D_MODEL = 2048
BRANCH_WIDTH = 1024
N_BRANCHES = 4
EPS = 1e-6

SSD_HEAD_DIM = 64
SSD_HEADS = 16
SSD_GROUPS = 2
SSD_STATE = 128
SSD_CONV = 4
SSD_CHUNK = 128
SSD_XBC = 1536

FOX_HEAD_DIM = 64
FOX_HEADS = 16

GLA_HEADS = 4
GLA_KEY_DIM = 128
GLA_VAL_DIM = 256
GLA_GATE_RANK = 16
GLA_GATE_TAU = 16.0
GLA_CHUNK = 64
GLA_BAND = 8

MLA_HEADS = 8
MLA_NOPE = 128
MLA_ROPE = 64
MLA_V = 128
MLA_Q_RANK = 512
MLA_KV_RANK = 512
MLA_QK_PAD = 256
ROPE_BASE = 10000.0

LANES = 128
VMEM_LIMIT = 56 * 1024 * 1024

_IN_NAMES = ("m_z", "m_xbc", "m_dt", "f_q", "f_k", "f_v", "f_f", "f_gate",
             "g_q", "g_k", "g_v", "g_lr", "g_gate", "l_cq", "l_ckv", "l_kr", "l_gate", "merge")
_IN_SIZES = (1024, 1536, 16, 1024, 1024, 1024, 16, 1024,
             512, 512, 1024, 16, 1024, 512, 512, 64, 1024, 8192)
_BIG_ORDER = ("merge", "m_z", "f_q", "f_k", "f_v", "f_gate", "g_v", "g_gate", "l_gate",
              "g_q", "g_k", "l_cq", "l_ckv", "m_xbc")
_SIZE = dict(zip(_IN_NAMES, _IN_SIZES))
_OFF = {}
_o = 0
for _n in _BIG_ORDER:
    _OFF[_n] = _o
    _o += _SIZE[_n]
N_BIG = _o
N_SMALL = 3 * LANES
SM_DT, SM_FF, SM_LR = 0, 16, 32


def _cparams(sem):
    return pltpu.CompilerParams(dimension_semantics=sem, vmem_limit_bytes=VMEM_LIMIT)


def _silu(v):
    return v / (1.0 + jnp.exp(-v))


def _softplus(v):
    return jnp.maximum(v, 0.0) + jnp.log1p(jnp.exp(-jnp.abs(v)))


def _log_sigmoid(v):
    return jnp.minimum(v, 0.0) - jnp.log1p(jnp.exp(-jnp.abs(v)))


def _tril(n):
    r = lax.broadcasted_iota(jnp.int32, (n, n), 0)
    c = lax.broadcasted_iota(jnp.int32, (n, n), 1)
    return r >= c


def _dot(a, b):
    return jnp.dot(a, b, preferred_element_type=F32)


def _dot_nt(a, b):
    return lax.dot_general(a, b, (((1,), (1,)), ((), ())), preferred_element_type=F32)


def _dot_tn(a, b):
    return lax.dot_general(a, b, (((0,), (0,)), ((), ())), preferred_element_type=F32)


def _dot_hi(a, b):
    return jnp.dot(a, b, precision=HIGHEST, preferred_element_type=F32)


def _split3(v):
    hi = v.astype(BF16).astype(F32)
    mid = (v - hi).astype(BF16).astype(F32)
    lo = v - hi - mid
    return hi, mid, lo


def _dot_sel_lhs(sel, x):
    s = sel.astype(BF16)
    hi, mid, lo = _split3(x)
    return _dot(s, hi.astype(BF16)) + _dot(s, mid.astype(BF16)) + _dot(s, lo.astype(BF16))


def _dot_sel_rhs(x, sel):
    hi, mid, lo = _split3(x)
    return _dot(hi.astype(BF16), sel) + _dot(mid.astype(BF16), sel) + _dot(lo.astype(BF16), sel)


def _in_proj_kernel(x_ref, pn_ref, w_ref, ws_ref, o_ref, os_ref, h_ref):
    @pl.when(pl.program_id(1) == 0)
    def _():
        x = x_ref[...]
        ms = jnp.mean(x * x, axis=-1, keepdims=True)
        hb = (x * lax.rsqrt(ms + EPS) * pn_ref[...]).astype(BF16)
        h_ref[...] = hb
        os_ref[...] = _dot(hb, ws_ref[...])

    o_ref[...] = _dot(h_ref[...], w_ref[...]).astype(BF16)


def _in_proj(x2d, pre_norm, w_big, w_small, tm, tn):
    t = x2d.shape[0]
    return pl.pallas_call(
        _in_proj_kernel,
        grid=(t // tm, N_BIG // tn),
        in_specs=[
            pl.BlockSpec((tm, D_MODEL), lambda i, j: (i, 0)),
            pl.BlockSpec((1, D_MODEL), lambda i, j: (0, 0)),
            pl.BlockSpec((D_MODEL, tn), lambda i, j: (0, j)),
            pl.BlockSpec((D_MODEL, N_SMALL), lambda i, j: (0, 0)),
        ],
        out_specs=[
            pl.BlockSpec((tm, tn), lambda i, j: (i, j)),
            pl.BlockSpec((tm, N_SMALL), lambda i, j: (i, 0)),
        ],
        out_shape=[
            jax.ShapeDtypeStruct((t, N_BIG), BF16),
            jax.ShapeDtypeStruct((t, N_SMALL), F32),
        ],
        scratch_shapes=[pltpu.VMEM((tm, D_MODEL), BF16)],
        compiler_params=_cparams(("parallel", "arbitrary")),
        name="in_proj",
    )(x2d, pre_norm.reshape(1, D_MODEL), w_big, w_small)


def _fcum_kernel(s_ref, fb_ref, o_ref, carry_ref, *, tc):
    @pl.when(pl.program_id(1) == 0)
    def _():
        carry_ref[...] = jnp.zeros_like(carry_ref)

    lf = _log_sigmoid(s_ref[...] + fb_ref[...])
    cum = _dot_sel_lhs(_tril(tc), lf) + carry_ref[...]
    o_ref[...] = cum
    carry_ref[...] = cum[tc - 1:tc, :]


def _fcum(small, fgate_b, bsz, s, tc):
    fb = jnp.zeros((1, LANES), F32).at[0, SM_FF:SM_FF + FOX_HEADS].set(fgate_b.astype(F32))
    nj = s // tc
    return pl.pallas_call(
        functools.partial(_fcum_kernel, tc=tc),
        grid=(bsz, nj),
        in_specs=[
            pl.BlockSpec((tc, LANES), lambda b, j: (b * nj + j, 0)),
            pl.BlockSpec((1, LANES), lambda b, j: (0, 0)),
        ],
        out_specs=pl.BlockSpec((tc, LANES), lambda b, j: (b * nj + j, 0)),
        out_shape=jax.ShapeDtypeStruct((bsz * s, LANES), F32),
        scratch_shapes=[pltpu.VMEM((1, LANES), F32)],
        compiler_params=_cparams(("parallel", "arbitrary")),
        name="fcum",
    )(small, fb)


def _ssd_kernel(z_ref, xbc_ref, sm_ref, cw_ref, cb_ref, dtb_ref, alog_ref, e_ref, dsk_ref,
                nw_ref, o_ref, tail_ref, state_ref, y_ref):
    L = SSD_CHUNK
    gw = BRANCH_WIDTH // SSD_GROUPS
    hpg = SSD_HEADS // SSD_GROUPS

    @pl.when(pl.program_id(1) == 0)
    def _():
        tail_ref[...] = jnp.zeros_like(tail_ref)
        state_ref[...] = jnp.zeros_like(state_ref)

    cur = xbc_ref[...].astype(F32)
    xcat = jnp.concatenate([tail_ref[...], cur], axis=0)
    w = cw_ref[...]
    y = cb_ref[...]
    for k in range(SSD_CONV):
        y = y + w[k:k + 1, :] * xcat[5 + k:5 + k + L, :]
    tail_ref[...] = cur[L - 8:L, :]
    xbc = _silu(y)
    xs = xbc[:, :BRANCH_WIDTH]
    bm = xbc[:, BRANCH_WIDTH:BRANCH_WIDTH + SSD_GROUPS * SSD_STATE]
    cm = xbc[:, BRANCH_WIDTH + SSD_GROUPS * SSD_STATE:]

    lane = lax.broadcasted_iota(jnp.int32, (L, LANES), 1)
    head_lane = lane < SSD_HEADS
    dt = jnp.where(head_lane, _softplus(sm_ref[...] + dtb_ref[...]), 0.0)
    la = dt * (-jnp.exp(alog_ref[...]))
    causal = _tril(L)
    acum = _dot_sel_lhs(causal, la)
    acum_t = acum.T
    a_last = acum[L - 1:L, :]
    e = e_ref[...]
    dt_e = _dot_sel_rhs(dt, e)
    eac_e = jnp.exp(_dot_sel_rhs(acum, e))
    dte_e = jnp.exp(_dot_sel_rhs(a_last - acum, e))
    xdt = xs * dt_e
    xw = (xdt * dte_e).astype(BF16)
    xdt_b = xdt.astype(BF16)

    yoff = []
    for g in range(SSD_GROUPS):
        bg = bm[:, g * SSD_STATE:(g + 1) * SSD_STATE].astype(BF16)
        cg = cm[:, g * SSD_STATE:(g + 1) * SSD_STATE].astype(BF16)
        cb = _dot_nt(cg, bg)
        st = state_ref[:, g * gw:(g + 1) * gw]
        yoff.append(_dot(cg, st.astype(BF16)) * eac_e[:, g * gw:(g + 1) * gw])
        state_ref[:, g * gw:(g + 1) * gw] = (
            st * eac_e[L - 1:L, g * gw:(g + 1) * gw] + _dot_tn(bg, xw[:, g * gw:(g + 1) * gw]))
        for r in range(hpg):
            h = g * hpg + r
            seg = acum[:, h:h + 1] - acum_t[h:h + 1, :]
            dec = jnp.where(causal, jnp.exp(jnp.minimum(seg, 0.0)), 0.0)
            m = (cb * dec).astype(BF16)
            y_ref[:, h * SSD_HEAD_DIM:(h + 1) * SSD_HEAD_DIM] = _dot(
                m, xdt_b[:, h * SSD_HEAD_DIM:(h + 1) * SSD_HEAD_DIM])

    yt = y_ref[...] + jnp.concatenate(yoff, axis=1) + xs * dsk_ref[...]
    yt = yt * _silu(z_ref[...].astype(F32))
    nw = nw_ref[...]
    for g in range(SSD_GROUPS):
        yg = yt[:, g * gw:(g + 1) * gw]
        ms = jnp.mean(yg * yg, axis=-1, keepdims=True)
        o_ref[:, g * gw:(g + 1) * gw] = (
            yg * lax.rsqrt(ms + EPS) * nw[:, g * gw:(g + 1) * gw]).astype(BF16)


def _ssd(proj, small, conv_w, conv_b, dt_bias, a_log, d_skip, ssm_norm, bsz, s):
    L = SSD_CHUNK
    nc = s // L
    t = bsz * s
    pad = lambda v: jnp.zeros((1, LANES), F32).at[0, :SSD_HEADS].set(v.astype(F32))
    expand = jnp.zeros((LANES, BRANCH_WIDTH), BF16).at[:SSD_HEADS].set(
        jnp.repeat(jnp.eye(SSD_HEADS, dtype=BF16), SSD_HEAD_DIM, axis=1))
    dsk = jnp.repeat(d_skip.astype(F32), SSD_HEAD_DIM).reshape(1, BRANCH_WIDTH)
    row = lambda b, c: (b * nc + c, 0)
    const = lambda b, c: (0, 0)
    return pl.pallas_call(
        _ssd_kernel,
        grid=(bsz, nc),
        in_specs=[
            pl.BlockSpec((L, BRANCH_WIDTH), lambda b, c: (b * nc + c, _OFF["m_z"] // BRANCH_WIDTH)),
            pl.BlockSpec((L, SSD_XBC), lambda b, c: (b * nc + c, _OFF["m_xbc"] // SSD_XBC)),
            pl.BlockSpec((L, LANES), row),
            pl.BlockSpec((SSD_CONV, SSD_XBC), const),
            pl.BlockSpec((1, SSD_XBC), const),
            pl.BlockSpec((1, LANES), const),
            pl.BlockSpec((1, LANES), const),
            pl.BlockSpec((LANES, BRANCH_WIDTH), const),
            pl.BlockSpec((1, BRANCH_WIDTH), const),
            pl.BlockSpec((1, BRANCH_WIDTH), const),
        ],
        out_specs=pl.BlockSpec((L, BRANCH_WIDTH), row),
        out_shape=jax.ShapeDtypeStruct((t, BRANCH_WIDTH), BF16),
        scratch_shapes=[
            pltpu.VMEM((8, SSD_XBC), F32),
            pltpu.VMEM((SSD_STATE, BRANCH_WIDTH), F32),
            pltpu.VMEM((L, BRANCH_WIDTH), F32),
        ],
        compiler_params=_cparams(("parallel", "arbitrary")),
        name="ssd",
    )(proj, proj, small, conv_w.astype(F32), conv_b.astype(F32).reshape(1, SSD_XBC),
      pad(dt_bias), pad(a_log), expand, dsk, ssm_norm.astype(F32).reshape(1, BRANCH_WIDTH))


NEG_BIG = -1e30
LOG2E = 1.4426950408889634
ONES_ROWS = 16


def _flash_kernel(*refs, nsub, dk, dv, tq, tk, seq, has_bias, q_scale, head_lane0):
    if has_bias:
        q_ref, k_ref, v_ref, fq_ref, fs_ref, g_ref, o_ref, vt_ref, ka_ref = refs
    else:
        q_ref, k_ref, v_ref, g_ref, o_ref, vt_ref = refs
    hp = pl.program_id(1)
    qi = pl.program_id(2)
    ratio = tq // tk
    dvp = dv + ONES_ROWS

    def with_bias_lanes(x, f_col, s, query_side):
        rows = x.shape[0]
        lane = lax.broadcasted_iota(jnp.int32, (rows, LANES), 1)
        own = (lane >= s * dk) & (lane < (s + 1) * dk)
        c0 = (1 - s) * dk
        hi, mid, lo = _split3(f_col * LOG2E)
        one = jnp.ones_like(f_col)
        pieces = (hi, mid, lo, one, one, one) if query_side else (one, one, one, -hi, -mid, -lo)
        extra = jnp.zeros((rows, LANES), F32)
        for j, pc in enumerate(pieces):
            extra = jnp.where(lane == c0 + j, pc, extra)
        return jnp.where(own, x, extra).astype(BF16)

    def pick_head(f, s):
        lane = lax.broadcasted_iota(jnp.int32, f.shape, 1)
        return jnp.sum(jnp.where(lane == head_lane0 + hp * nsub + s, f, 0.0), axis=1,
                       keepdims=True)

    @pl.when(qi == 0)
    def _():
        for c in range(seq // tk):
            rows = slice(c * tk, (c + 1) * tk)
            vt = v_ref[rows, :].astype(F32).T.astype(BF16)
            for s in range(nsub):
                vt_ref[c, s * dvp:s * dvp + dv, :] = vt[s * dv:(s + 1) * dv, :]
                vt_ref[c, s * dvp + dv:(s + 1) * dvp, :] = jnp.ones((dvp - dv, tk), BF16)
            if has_bias:
                kf = k_ref[rows, :].astype(F32)
                f = fs_ref[rows, :]
                for s in range(nsub):
                    ka_ref[s, rows, :] = with_bias_lanes(kf, pick_head(f, s), s, False)

    if has_bias:
        qf = q_ref[...].astype(F32) * (q_scale * LOG2E)
        fq = fq_ref[...]
        qa = [with_bias_lanes(qf, pick_head(fq, s), s, True) for s in range(nsub)]
    else:
        qa = [q_ref[:, s * dk:(s + 1) * dk] for s in range(nsub)]

    def block(kj, carry, lane0, masked):
        w = tq - lane0
        out = []
        for s in range(nsub):
            m, acc = carry[s]
            rows = pl.ds(pl.multiple_of(kj * tk, tk), tk)
            kb = ka_ref[s, rows, :] if has_bias else k_ref[rows, s * dk:(s + 1) * dk]
            st = _dot_nt(kb, qa[s][lane0:, :])
            if masked:
                key = lax.broadcasted_iota(jnp.int32, (tk, w), 0)
                qry = lax.broadcasted_iota(jnp.int32, (tk, w), 1)
                st = jnp.where(key <= qry, st, NEG_BIG)
            m_old = m[:, lane0:]
            m_new = jnp.maximum(m_old, jnp.max(st, axis=0, keepdims=True))
            p = jnp.exp2(st - m_new)
            alpha = jnp.exp2(m_old - m_new)
            vt = vt_ref[kj, s * dvp:(s + 1) * dvp, :]
            acc_new = alpha * acc[:, lane0:] + _dot(vt, p.astype(BF16))
            if lane0:
                m_new = jnp.concatenate([m[:, :lane0], m_new], axis=1)
                acc_new = jnp.concatenate([acc[:, :lane0], acc_new], axis=1)
            out.append((m_new, acc_new))
        return tuple(out)

    init = tuple((jnp.full((1, tq), NEG_BIG, F32), jnp.zeros((dvp, tq), F32))
                 for _ in range(nsub))
    carry = lax.fori_loop(0, qi * ratio, lambda j, c: block(j, c, 0, False), init)
    for d in range(ratio):
        carry = block(qi * ratio + d, carry, d * tk, True)
    ot = jnp.concatenate([acc[:dv] / acc[dv:dv + 1] for (_, acc) in carry],
                         axis=0)
    o_ref[...] = (ot.T * _silu(g_ref[...].astype(F32))).astype(BF16)


def _flash(q_arr, q_col0, k_arr, k_col0, v_arr, v_col0, gate_arr, gate_col0, bsz, s, tq, tk,
           nsub, dk, dv, q_scale, fcum=None):
    nq = s // tq
    nh = BRANCH_WIDTH // (nsub * dv)
    qmap = lambda c0: (lambda b, h, i: (b * nq + i, c0 + h))
    kvmap = lambda c0: (lambda b, h, i: (b, c0 + h))
    in_specs = [
        pl.BlockSpec((tq, nsub * dk), qmap(q_col0)),
        pl.BlockSpec((s, nsub * dk), kvmap(k_col0)),
        pl.BlockSpec((s, nsub * dv), kvmap(v_col0)),
    ]
    args = [q_arr, k_arr, v_arr]
    scratch = [pltpu.VMEM((s // tk, nsub * (dv + ONES_ROWS), tk), BF16)]
    if fcum is not None:
        in_specs += [
            pl.BlockSpec((tq, LANES), lambda b, h, i: (b * nq + i, 0)),
            pl.BlockSpec((s, LANES), lambda b, h, i: (b, 0)),
        ]
        args += [fcum, fcum]
        scratch.append(pltpu.VMEM((nsub, s, LANES), BF16))
    in_specs.append(pl.BlockSpec((tq, nsub * dv), qmap(gate_col0)))
    args.append(gate_arr)
    return pl.pallas_call(
        functools.partial(_flash_kernel, nsub=nsub, dk=dk, dv=dv, tq=tq, tk=tk, seq=s,
                          has_bias=fcum is not None, q_scale=q_scale, head_lane0=SM_FF),
        grid=(bsz, nh, nq),
        in_specs=in_specs,
        out_specs=pl.BlockSpec((tq, nsub * dv), lambda b, h, i: (b * nq + i, h)),
        out_shape=jax.ShapeDtypeStruct((bsz * s, BRANCH_WIDTH), BF16),
        scratch_shapes=scratch,
        compiler_params=_cparams(("parallel", "parallel", "arbitrary")),
        name="flash_fox" if fcum is not None else "flash_mla",
    )(*args)


def _gla_kernel(q_ref, k_ref, v_ref, g_ref, sm_ref, w2_ref, b2_ref, nw_ref, o_ref, st_ref):
    C, H, DK, DV = GLA_CHUNK, GLA_HEADS, GLA_KEY_DIM, GLA_VAL_DIM

    @pl.when(pl.program_id(1) == 0)
    def _():
        st_ref[...] = jnp.zeros_like(st_ref)

    la = _log_sigmoid(_dot_hi(sm_ref[...], w2_ref[...]) + b2_ref[...]) * (1.0 / GLA_GATE_TAU)
    bcum = _dot_sel_lhs(_tril(C), la)
    q = q_ref[...].astype(F32) * (DK ** -0.5)
    k = k_ref[...].astype(F32)
    b_last = bcum[C - 1:C, :]
    qe = (q * jnp.exp(bcum)).astype(BF16)
    kd = (k * jnp.exp(b_last - bcum)).astype(BF16)
    decay_all = jnp.exp(b_last)

    rowi = lax.broadcasted_iota(jnp.int32, (C, C), 0)
    coli = lax.broadcasted_iota(jnp.int32, (C, C), 1)
    dist = rowi - coli
    attn = [jnp.zeros((C, C), F32) for _ in range(H)]
    for d in range(GLA_BAND):
        kr = k if d == 0 else pltpu.roll(k, d, 0)
        br = bcum if d == 0 else pltpu.roll(bcum, d, 0)
        term = q * kr * jnp.exp(jnp.minimum(bcum - br, 0.0))
        for h in range(H):
            dd = jnp.sum(term[:, h * DK:(h + 1) * DK], axis=-1, keepdims=True)
            attn[h] = attn[h] + jnp.where(dist == d, dd, 0.0)
    far = [[jnp.zeros((GLA_BAND, C), F32)] for _ in range(H)]
    for i0 in range(GLA_BAND, C, GLA_BAND):
        bref = bcum[i0 - 1:i0, :]
        qi = (q[i0:i0 + GLA_BAND] * jnp.exp(bcum[i0:i0 + GLA_BAND] - bref)).astype(BF16)
        kj = (k * jnp.exp(jnp.minimum(bref - bcum, 0.0))).astype(BF16)
        for h in range(H):
            far[h].append(_dot_nt(qi[:, h * DK:(h + 1) * DK], kj[:, h * DK:(h + 1) * DK]))

    nw = nw_ref[...]
    for h in range(H):
        ks, vs = slice(h * DK, (h + 1) * DK), slice(h * DV, (h + 1) * DV)
        a = attn[h] + jnp.where(dist >= GLA_BAND, jnp.concatenate(far[h], axis=0), 0.0)
        v = v_ref[:, vs]
        st = st_ref[h]
        o = _dot_nt(qe[:, ks], st.astype(BF16)) + _dot(a.astype(BF16), v)
        st_ref[h] = st * decay_all[:, ks] + _dot_tn(v, kd[:, ks])
        ms = jnp.mean(o * o, axis=-1, keepdims=True)
        on = o * lax.rsqrt(ms + EPS) * nw
        o_ref[:, vs] = (on * _silu(g_ref[:, vs].astype(F32))).astype(BF16)


def _gla(proj, small, gla_w2, gla_b, gla_norm, bsz, s):
    C = GLA_CHUNK
    nc = s // C
    t = bsz * s
    hk = GLA_HEADS * GLA_KEY_DIM
    w2 = jnp.zeros((LANES, hk), F32).at[SM_LR:SM_LR + GLA_GATE_RANK].set(gla_w2.astype(F32))
    b2 = gla_b.astype(F32).reshape(1, hk)
    cmap = lambda name, w: (lambda b, c: (b * nc + c, _OFF[name] // w))
    const = lambda b, c: (0, 0)
    return pl.pallas_call(
        _gla_kernel,
        grid=(bsz, nc),
        in_specs=[
            pl.BlockSpec((C, hk), cmap("g_q", hk)),
            pl.BlockSpec((C, hk), cmap("g_k", hk)),
            pl.BlockSpec((C, BRANCH_WIDTH), cmap("g_v", BRANCH_WIDTH)),
            pl.BlockSpec((C, BRANCH_WIDTH), cmap("g_gate", BRANCH_WIDTH)),
            pl.BlockSpec((C, LANES), lambda b, c: (b * nc + c, 0)),
            pl.BlockSpec((LANES, hk), const),
            pl.BlockSpec((1, hk), const),
            pl.BlockSpec((1, GLA_VAL_DIM), const),
        ],
        out_specs=pl.BlockSpec((C, BRANCH_WIDTH), lambda b, c: (b * nc + c, 0)),
        out_shape=jax.ShapeDtypeStruct((t, BRANCH_WIDTH), BF16),
        scratch_shapes=[pltpu.VMEM((GLA_HEADS, GLA_VAL_DIM, GLA_KEY_DIM), F32)],
        compiler_params=_cparams(("parallel", "arbitrary")),
        name="gla",
    )(proj, proj, proj, proj, small, w2, b2, gla_norm.astype(F32).reshape(1, GLA_VAL_DIM))


def _mla_prep_kernel(cq_ref, ckv_ref, kr_ref, krs_ref, pos_ref, invf_ref, sign_ref, qn_ref,
                     kvn_ref, wqn_ref, wqr_ref, wqs_ref, wk_ref, wv_ref, q_ref, k_ref, v_ref):
    def norm(ref, w_ref):
        c = ref[...].astype(F32)
        ms = jnp.mean(c * c, axis=-1, keepdims=True)
        return (c * lax.rsqrt(ms + EPS) * w_ref[...]).astype(BF16)

    cq = norm(cq_ref, qn_ref)
    ckv = norm(ckv_ref, kvn_ref)
    ang = pos_ref[...] * invf_ref[...]
    cos = jnp.cos(ang)
    sin = jnp.sin(ang) * sign_ref[...]
    scale = (MLA_NOPE + MLA_ROPE) ** -0.5 * LOG2E

    qn = _dot(cq, wqn_ref[...]) * scale
    qr = _dot(cq, wqr_ref[...])
    qsw = _dot(cq, wqs_ref[...])
    kn = _dot(ckv, wk_ref[...])
    v_ref[...] = _dot(ckv, wv_ref[...]).astype(BF16)
    kr = (kr_ref[...] * cos + krs_ref[...] * sin).astype(BF16)
    for h in range(MLA_HEADS):
        lo, hi = h * LANES, (h + 1) * LANES
        base = h * MLA_QK_PAD
        q_ref[:, base:base + LANES] = qn[:, lo:hi].astype(BF16)
        q_ref[:, base + LANES:base + 2 * LANES] = (
            (qr[:, lo:hi] * cos + qsw[:, lo:hi] * sin) * scale).astype(BF16)
        k_ref[:, base:base + LANES] = kn[:, lo:hi].astype(BF16)
        k_ref[:, base + LANES:base + 2 * LANES] = kr


def _mla_prep(proj, small, posf, q_norm, kv_norm, w_uq, w_ukv, tm):
    t = proj.shape[0]
    half = MLA_ROPE // 2
    wq = w_uq.reshape(MLA_Q_RANK, MLA_HEADS, MLA_NOPE + MLA_ROPE)
    zpad = jnp.zeros((MLA_Q_RANK, MLA_HEADS, LANES - MLA_ROPE), w_uq.dtype)
    flat = lambda a: a.reshape(MLA_Q_RANK, MLA_HEADS * LANES).astype(BF16)
    wqn = flat(wq[:, :, :MLA_NOPE])
    wqr = flat(jnp.concatenate([wq[:, :, MLA_NOPE:], zpad], axis=2))
    wqs = flat(jnp.concatenate([wq[:, :, MLA_NOPE + half:], wq[:, :, MLA_NOPE:MLA_NOPE + half],
                                zpad], axis=2))
    wkv = w_ukv.reshape(MLA_KV_RANK, MLA_HEADS, MLA_NOPE + MLA_V)
    wk = flat(wkv[:, :, :MLA_NOPE])
    wv = flat(wkv[:, :, MLA_NOPE:])
    inv_freq = 1.0 / (ROPE_BASE ** (jnp.arange(half, dtype=F32) * 2.0 / MLA_ROPE))
    zero = jnp.zeros((LANES - MLA_ROPE,), F32)
    invf = jnp.concatenate([inv_freq, inv_freq, zero]).reshape(1, LANES)
    sign = jnp.concatenate([-jnp.ones((half,), F32), jnp.ones((half,), F32), zero]).reshape(1, LANES)
    wide = MLA_HEADS * MLA_QK_PAD
    row = lambda c: (lambda i: (i, c))
    const = lambda i: (0, 0)
    wspec = pl.BlockSpec((MLA_Q_RANK, MLA_HEADS * LANES), const)
    return pl.pallas_call(
        _mla_prep_kernel,
        grid=(t // tm,),
        in_specs=[
            pl.BlockSpec((tm, MLA_Q_RANK), row(_OFF["l_cq"] // MLA_Q_RANK)),
            pl.BlockSpec((tm, MLA_KV_RANK), row(_OFF["l_ckv"] // MLA_KV_RANK)),
            pl.BlockSpec((tm, LANES), row(1)),
            pl.BlockSpec((tm, LANES), row(2)),
            pl.BlockSpec((tm, 1), row(0)),
            pl.BlockSpec((1, LANES), const),
            pl.BlockSpec((1, LANES), const),
            pl.BlockSpec((1, MLA_Q_RANK), const),
            pl.BlockSpec((1, MLA_KV_RANK), const),
            wspec, wspec, wspec, wspec, wspec,
        ],
        out_specs=[
            pl.BlockSpec((tm, wide), row(0)),
            pl.BlockSpec((tm, wide), row(0)),
            pl.BlockSpec((tm, BRANCH_WIDTH), row(0)),
        ],
        out_shape=[
            jax.ShapeDtypeStruct((t, wide), BF16),
            jax.ShapeDtypeStruct((t, wide), BF16),
            jax.ShapeDtypeStruct((t, BRANCH_WIDTH), BF16),
        ],
        compiler_params=_cparams(("parallel",)),
        name="mla_prep",
    )(proj, proj, small, small, posf, invf, sign, q_norm.astype(F32).reshape(1, MLA_Q_RANK),
      kv_norm.astype(F32).reshape(1, MLA_KV_RANK), wqn, wqr, wqs, wk, wv)


def _merge_kernel(a_ref, b_ref, c_ref, d_ref, g0_ref, g1_ref, g2_ref, g3_ref, w_ref, o_ref):
    acc = None
    for br, (x_ref, g_ref) in enumerate(
            ((a_ref, g0_ref), (b_ref, g1_ref), (c_ref, g2_ref), (d_ref, g3_ref))):
        gate = 1.0 / (1.0 + jnp.exp(-g_ref[...].astype(F32)))
        term = gate * _dot(x_ref[...], w_ref[br])
        acc = term if acc is None else acc + term
    o_ref[...] = acc.astype(BF16)


def _merge(outs, proj, w_branch, tm, tn):
    t = proj.shape[0]
    nn = D_MODEL // tn
    xspec = pl.BlockSpec((tm, BRANCH_WIDTH), lambda i, n: (i, 0))
    gspec = lambda br: pl.BlockSpec((tm, tn), lambda i, n: (i, _OFF["merge"] // tn + br * nn + n))
    return pl.pallas_call(
        _merge_kernel,
        grid=(t // tm, nn),
        in_specs=[xspec] * 4 + [gspec(br) for br in range(N_BRANCHES)] + [
            pl.BlockSpec((N_BRANCHES, BRANCH_WIDTH, tn), lambda i, n: (0, 0, n))],
        out_specs=pl.BlockSpec((tm, tn), lambda i, n: (i, n)),
        out_shape=jax.ShapeDtypeStruct((t, D_MODEL), BF16),
        compiler_params=_cparams(("parallel", "arbitrary")),
        name="merge",
    )(*outs, proj, proj, proj, proj, w_branch.astype(BF16))


def _out_proj_kernel(m_ref, x_ref, w_ref, nw_ref, o_ref):
    y = _dot(m_ref[...], w_ref[...])
    ms = jnp.mean(y * y, axis=-1, keepdims=True)
    o_ref[...] = x_ref[...] + y * lax.rsqrt(ms + EPS) * nw_ref[...]


def _out_proj(mixed, x2d, w_out, post_norm, tm):
    t = x2d.shape[0]
    return pl.pallas_call(
        _out_proj_kernel,
        grid=(t // tm,),
        in_specs=[
            pl.BlockSpec((tm, D_MODEL), lambda i: (i, 0)),
            pl.BlockSpec((tm, D_MODEL), lambda i: (i, 0)),
            pl.BlockSpec((D_MODEL, D_MODEL), lambda i: (0, 0)),
            pl.BlockSpec((1, D_MODEL), lambda i: (0, 0)),
        ],
        out_specs=pl.BlockSpec((tm, D_MODEL), lambda i: (i, 0)),
        out_shape=jax.ShapeDtypeStruct((t, D_MODEL), F32),
        compiler_params=_cparams(("parallel",)),
        name="out_proj",
    )(mixed, x2d, w_out.astype(BF16), post_norm.astype(F32).reshape(1, D_MODEL))


def _pack_w_in(w_in):
    bounds = np.cumsum((0,) + _IN_SIZES)
    seg = {n: w_in[:, bounds[i]:bounds[i + 1]] for i, n in enumerate(_IN_NAMES)}
    big = jnp.concatenate([seg[n] for n in _BIG_ORDER], axis=1).astype(BF16)
    half = MLA_ROPE // 2
    kr = seg["l_kr"]
    z = lambda n: jnp.zeros((D_MODEL, n), w_in.dtype)
    small = jnp.concatenate([
        seg["m_dt"], seg["f_f"], seg["g_lr"], z(LANES - 48),
        kr, z(LANES - MLA_ROPE),
        kr[:, half:], kr[:, :half], z(LANES - MLA_ROPE)], axis=1).astype(BF16)
    return big, small


def _tile(n, want):
    return want if n % want == 0 else n


def _layer(x2d, posf, bsz, s, pre_norm, post_norm, w_in, conv_w, conv_b, dt_bias, a_log, d_skip,
           ssm_norm, fgate_b, gla_w2, gla_b, gla_norm, q_norm, kv_norm, w_uq, w_ukv, w_branch,
           w_out):
    t = bsz * s
    w_big, w_small = _pack_w_in(w_in)
    proj, small = _in_proj(x2d, pre_norm.astype(F32), w_big, w_small, _tile(t, 1024), 1536)

    out_a = _ssd(proj, small, conv_w, conv_b, dt_bias, a_log, d_skip, ssm_norm, bsz, s)

    fcum = _fcum(small, fgate_b, bsz, s, _tile(s, 512))
    tq, tk = _tile(s, 2048), _tile(s, 512)
    pair = 2 * FOX_HEAD_DIM
    out_b = _flash(proj, _OFF["f_q"] // pair, proj, _OFF["f_k"] // pair, proj, _OFF["f_v"] // pair,
                   proj, _OFF["f_gate"] // pair, bsz, s, tq, tk, nsub=2, dk=FOX_HEAD_DIM,
                   dv=FOX_HEAD_DIM, q_scale=FOX_HEAD_DIM ** -0.5, fcum=fcum)

    out_c = _gla(proj, small, gla_w2, gla_b, gla_norm, bsz, s)

    q_full, k_full, v_mla = _mla_prep(proj, small, posf, q_norm, kv_norm, w_uq, w_ukv,
                                      _tile(t, 512))
    out_d = _flash(q_full, 0, k_full, 0, v_mla, 0, proj, _OFF["l_gate"] // (2 * MLA_V), bsz, s,
                   tq, tk, nsub=2, dk=MLA_QK_PAD, dv=MLA_V, q_scale=1.0)

    mixed = _merge((out_a, out_b, out_c, out_d), proj, w_branch, _tile(t, 1024), 512)
    return _out_proj(mixed, x2d, w_out, post_norm, _tile(t, 512))


def kernel(x, positions, pre_norm, post_norm, w_in, conv_w, conv_b, dt_bias, a_log, d_skip,
           ssm_norm, fgate_b, gla_w2, gla_b, gla_norm, q_norm, kv_norm, w_uq, w_ukv, w_branch,
           w_out):
    bsz, s, _ = x.shape
    x2d = x.reshape(bsz * s, D_MODEL)
    posf = positions.astype(F32).reshape(bsz * s, 1)
    for l in range(pre_norm.shape[0]):
        x2d = _layer(x2d, posf, bsz, s, pre_norm[l], post_norm[l], w_in[l], conv_w[l], conv_b[l],
                     dt_bias[l], a_log[l], d_skip[l], ssm_norm[l], fgate_b[l], gla_w2[l], gla_b[l],
                     gla_norm[l], q_norm[l], kv_norm[l], w_uq[l], w_ukv[l], w_branch[l], w_out[l])
    return x2d.reshape(bsz, s, D_MODEL)
```
